```python
import math
import jax
import jax.numpy as jnp
from jax import lax
import numpy as np

D_MODEL = 1024
BATCH = 8
SEQ = 4096
DEPTH = 4

CHUNK = 64
Q_BLOCK = 128
NORM_EPS = 1e-6
CONV_K = 4
D_FF = 2816
N_MEM = 256

SSD_HEADS = 8
SSD_HEAD_DIM = 64
SSD_WIDTH = SSD_HEADS * SSD_HEAD_DIM
SSD_GROUPS = 2
SSD_STATE = 64
SSD_CONV_DIM = SSD_WIDTH + 2 * SSD_GROUPS * SSD_STATE
SSD_IN = SSD_WIDTH + SSD_CONV_DIM + SSD_HEADS

MLA_HEADS = 4
MLA_Q_LORA = 256
MLA_KV_LORA = 128
MLA_NOPE = 64
MLA_ROPE = 32
MLA_V = 64
MLA_WIDTH = MLA_HEADS * MLA_V
MLA_IN = MLA_Q_LORA + MLA_KV_LORA + MLA_ROPE
ROPE_THETA = 10000.0

GDN_HEADS = 4
GDN_DK = 64
GDN_DV = 64
GDN_QK_WIDTH = GDN_HEADS * GDN_DK
GDN_WIDTH = GDN_HEADS * GDN_DV
GDN_CONV_DIM = 2 * GDN_QK_WIDTH + GDN_WIDTH
GDN_IN = GDN_CONV_DIM + GDN_WIDTH + 2 * GDN_HEADS

D_MIX = SSD_WIDTH + MLA_WIDTH + GDN_WIDTH
IN_COLS = SSD_IN + MLA_IN + GDN_IN

XA_HEADS = 4
XA_HEAD_DIM = D_MODEL // XA_HEADS

N_NORMS = 9
FFN1_PRE = 0
FFN1_POST = 1
MIX_PRE = 2
MIX_POST = 3
MEM_NORM = 4
XA_PRE = 5
XA_POST = 6
FFN2_PRE = 7
FFN2_POST = 8

kernel_name = 'hybrid_ssd_mla_gdn_macaron_trunk'


def rms_norm(x, g):
    xf = x.astype(jnp.float32)
    y = xf * lax.rsqrt(jnp.mean(xf * xf, axis=-1, keepdims=True) + NORM_EPS)
    return (y * g.astype(jnp.float32)).astype(x.dtype)


def l2_normalize(x):
    return x * lax.rsqrt(jnp.sum(x * x, axis=-1, keepdims=True) + NORM_EPS)


def swiglu_ffn(x, w_up, w_down):
    gate, up = jnp.split(x @ w_up, 2, axis=-1)
    return (jax.nn.silu(gate) * up) @ w_down


def causal_depthwise_conv(x, w):
    k = w.shape[0]
    xp = jnp.pad(x, ((0, 0), (k - 1, 0), (0, 0)))
    return lax.conv_general_dilated(xp, w[:, None, :].astype(x.dtype), (1,), 'VALID',
                                    dimension_numbers=('NWC', 'WIO', 'NWC'),
                                    feature_group_count=x.shape[-1])


def rope_tables(positions):
    inv = 1.0 / (ROPE_THETA ** (jnp.arange(0, MLA_ROPE, 2, dtype=jnp.float32) / MLA_ROPE))
    ang = positions.astype(jnp.float32)[..., None] * inv
    return jnp.cos(ang), jnp.sin(ang)


def apply_rope(x, cos, sin):
    xf = x.astype(jnp.float32)
    x1, x2 = jnp.split(xf, 2, axis=-1)
    return jnp.concatenate([x1 * cos - x2 * sin, x2 * cos + x1 * sin], axis=-1).astype(x.dtype)


def ssd_mixer(cols, conv_w, conv_b, dt_bias, a_log, d_skip, norm_g):
    f32 = jnp.float32
    bsz, s_len, _ = cols.shape
    nc = s_len // CHUNK
    hpg = SSD_HEADS // SSD_GROUPS
    z, xbc, dt = jnp.split(cols, [SSD_WIDTH, SSD_WIDTH + SSD_CONV_DIM], axis=-1)
    xbc = jax.nn.silu(causal_depthwise_conv(xbc, conv_w) + conv_b)
    xs, bm, cm = jnp.split(xbc, [SSD_WIDTH, SSD_WIDTH + SSD_GROUPS * SSD_STATE], axis=-1)
    xs = xs.reshape(bsz, nc, CHUNK, SSD_GROUPS, hpg, SSD_HEAD_DIM).astype(f32)
    bm = bm.reshape(bsz, nc, CHUNK, SSD_GROUPS, SSD_STATE).astype(f32)
    cm = cm.reshape(bsz, nc, CHUNK, SSD_GROUPS, SSD_STATE).astype(f32)
    dt = jax.nn.softplus(dt.astype(f32) + dt_bias.astype(f32))
    a = -jnp.exp(a_log.astype(f32))
    dt = dt.reshape(bsz, nc, CHUNK, SSD_GROUPS, hpg)
    xdt = xs * dt[..., None]
    a_cum = jnp.cumsum(dt * a.reshape(SSD_GROUPS, hpg), axis=2)
    causal = jnp.tril(jnp.ones((CHUNK, CHUNK), bool))[:, :, None, None]
    seg = a_cum[:, :, :, None] - a_cum[:, :, None, :]
    lmat = jnp.exp(jnp.where(causal, seg, -jnp.inf))
    cb = jnp.einsum('bclgn,bcsgn->bclsg', cm, bm)
    y_diag = jnp.einsum('bclsg,bclsgh,bcsghp->bclghp', cb, lmat, xdt)
    decay_to_end = jnp.exp(a_cum[:, :, -1:] - a_cum)
    chunk_states = jnp.einsum('bclgn,bclgh,bclghp->bcghpn', bm, decay_to_end, xdt)
    chunk_decay = jnp.exp(a_cum[:, :, -1])

    def step(state, inp):
        s_c, d_c = inp
        return state * d_c[..., None, None] + s_c, state

    h0 = jnp.zeros((bsz, SSD_GROUPS, hpg, SSD_HEAD_DIM, SSD_STATE), f32)
    _, h_in = lax.scan(step, h0, (jnp.moveaxis(chunk_states, 1, 0), jnp.moveaxis(chunk_decay, 1, 0)))
    h_in = jnp.moveaxis(h_in, 0, 1)
    y_off = jnp.einsum('bclgn,bcghpn,bclgh->bclghp', cm, h_in, jnp.exp(a_cum))
    y = y_diag + y_off + xs * d_skip.astype(f32).reshape(SSD_GROUPS, hpg)[..., None]
    y = y.reshape(bsz, s_len, SSD_WIDTH) * jax.nn.silu(z.astype(f32))
    y = rms_norm(y.reshape(bsz, s_len, SSD_GROUPS, SSD_WIDTH // SSD_GROUPS),
                 norm_g.reshape(SSD_GROUPS, SSD_WIDTH // SSD_GROUPS))
    return y.reshape(bsz, s_len, SSD_WIDTH).astype(cols.dtype)


def mla_mixer(cols, cos, sin, q_norm_g, w_uq, kv_norm_g, w_ukv):
    bsz, s_len, _ = cols.shape
    c_q, c_kv, k_rope = jnp.split(cols, [MLA_Q_LORA, MLA_Q_LORA + MLA_KV_LORA], axis=-1)
    q = (rms_norm(c_q, q_norm_g) @ w_uq).reshape(bsz, s_len, MLA_HEADS, MLA_NOPE + MLA_ROPE)
    q_nope, q_rope = jnp.split(q, [MLA_NOPE], axis=-1)
    q_rope = apply_rope(q_rope, cos[:, :, None], sin[:, :, None])
    k_rope = apply_rope(k_rope, cos, sin)
    kv = (rms_norm(c_kv, kv_norm_g) @ w_ukv).reshape(bsz, s_len, MLA_HEADS, MLA_NOPE + MLA_V)
    k_nope, v = jnp.split(kv, [MLA_NOPE], axis=-1)
    scale = (MLA_NOPE + MLA_ROPE) ** -0.5
    outs = []
    for qb in range(s_len // Q_BLOCK):
        q0 = qb * Q_BLOCK
        kend = q0 + Q_BLOCK
        s = (jnp.einsum('bqhd,bkhd->bhqk', q_nope[:, q0:kend], k_nope[:, :kend])
             + jnp.einsum('bqhr,bkr->bhqk', q_rope[:, q0:kend], k_rope[:, :kend]))
        s = s.astype(jnp.float32) * scale
        q_chunk = (q0 + jnp.arange(Q_BLOCK)) // CHUNK
        k_chunk = jnp.arange(kend) // CHUNK
        s = jnp.where(k_chunk[None, :] <= q_chunk[:, None], s, -jnp.inf)
        p = jax.nn.softmax(s, axis=-1).astype(v.dtype)
        outs.append(jnp.einsum('bhqk,bkhd->bqhd', p, v[:, :kend]))
    return jnp.concatenate(outs, axis=1).reshape(bsz, s_len, MLA_WIDTH)


def gdn_mixer(cols, conv_w, dt_bias, a_log, norm_g):
    f32 = jnp.float32
    bsz, s_len, _ = cols.shape
    nc = s_len // CHUNK
    qkv, z, b_raw, a_raw = jnp.split(
        cols, [GDN_CONV_DIM, GDN_CONV_DIM + GDN_WIDTH, GDN_CONV_DIM + GDN_WIDTH + GDN_HEADS], axis=-1)
    qkv = jax.nn.silu(causal_depthwise_conv(qkv, conv_w)).astype(f32)
    q, k, v = jnp.split(qkv, [GDN_QK_WIDTH, 2 * GDN_QK_WIDTH], axis=-1)

    def to_chunks(t, d):
        return t.reshape(bsz, nc, CHUNK, GDN_HEADS, d).transpose(0, 1, 3, 2, 4)

    def heads_to_chunks(t):
        return t.reshape(bsz, nc, CHUNK, GDN_HEADS).transpose(0, 1, 3, 2)

    q = l2_normalize(to_chunks(q, GDN_DK)) * GDN_DK ** -0.5
    k = l2_normalize(to_chunks(k, GDN_DK))
    v = to_chunks(v, GDN_DV)
    beta = heads_to_chunks(jax.nn.sigmoid(b_raw.astype(f32)))
    g = -jnp.exp(a_log.astype(f32)) * jax.nn.softplus(a_raw.astype(f32) + dt_bias.astype(f32))
    g_cum = jnp.cumsum(heads_to_chunks(g), axis=-1)
    incl = jnp.tril(jnp.ones((CHUNK, CHUNK), bool))
    strict = jnp.tril(jnp.ones((CHUNK, CHUNK), bool), -1)
    gamma = jnp.exp(jnp.where(incl, g_cum[..., :, None] - g_cum[..., None, :], -jnp.inf))
    kb = k * beta[..., None]
    m = jnp.where(strict, jnp.einsum('bchld,bchsd->bchls', kb, k) * gamma, 0.0)
    eye = jnp.eye(CHUNK, dtype=f32)
    rhs = jnp.concatenate([v * beta[..., None], kb * jnp.exp(g_cum)[..., None]], axis=-1)
    sol = lax.linalg.triangular_solve(eye + m, rhs, left_side=True, lower=True, unit_diagonal=True)
    u, w = jnp.split(sol, [GDN_DV], axis=-1)
    qk = jnp.where(incl, jnp.einsum('bchld,bchsd->bchls', q, k) * gamma, 0.0)
    q_dec = q * jnp.exp(g_cum)[..., None]
    k_dec = k * jnp.exp(g_cum[..., -1:] - g_cum)[..., None]
    chunk_decay = jnp.exp(g_cum[..., -1])

    def step(state, inp):
        u_c, w_c, qk_c, qd_c, kd_c, dec_c = inp
        v_new = u_c - jnp.einsum('bhld,bhdv->bhlv', w_c, state)
        o_c = jnp.einsum('bhld,bhdv->bhlv', qd_c, state) + jnp.einsum('bhls,bhsv->bhlv', qk_c, v_new)
        state = state * dec_c[..., None, None] + jnp.einsum('bhld,bhlv->bhdv', kd_c, v_new)
        return state, o_c

    s0 = jnp.zeros((bsz, GDN_HEADS, GDN_DK, GDN_DV), f32)
    xs = (jnp.moveaxis(u, 1, 0), jnp.moveaxis(w, 1, 0), jnp.moveaxis(qk, 1, 0),
          jnp.moveaxis(q_dec, 1, 0), jnp.moveaxis(k_dec, 1, 0), jnp.moveaxis(chunk_decay, 1, 0))
    _, o = lax.scan(step, s0, xs)
    o = jnp.moveaxis(o, 0, 1).transpose(0, 1, 3, 2, 4).reshape(bsz, s_len, GDN_HEADS, GDN_DV)
    o = rms_norm(o, norm_g) * jax.nn.silu(z.astype(f32).reshape(bsz, s_len, GDN_HEADS, GDN_DV))
    return o.reshape(bsz, s_len, GDN_WIDTH).astype(cols.dtype)


def memory_cross_attention(u, mem_n, w_q, w_kv, w_o):
    bsz, s_len, _ = u.shape
    n_mem = mem_n.shape[1]
    q = (u @ w_q).reshape(bsz, s_len, XA_HEADS, XA_HEAD_DIM)
    k, v = jnp.split(mem_n @ w_kv, 2, axis=-1)
    k = k.reshape(bsz, n_mem, XA_HEADS, XA_HEAD_DIM)
    v = v.reshape(bsz, n_mem, XA_HEADS, XA_HEAD_DIM)
    s = jnp.einsum('bqhd,bmhd->bhqm', q, k).astype(jnp.float32) * XA_HEAD_DIM ** -0.5
    p = jax.nn.softmax(s, axis=-1).astype(v.dtype)
    o = jnp.einsum('bhqm,bmhd->bqhd', p, v).reshape(bsz, s_len, D_MODEL)
    return o @ w_o


def _dt_bias(k, shape):
    dt = jnp.exp(jax.random.uniform(k, shape, jnp.float32, math.log(1e-3), math.log(1e-1)))
    return dt + jnp.log(-jnp.expm1(-dt))


def setup_inputs(seed: int = 0) -> dict:
    key = jax.random.key(seed)
    ks = jax.random.split(key, 25)
    f32 = jnp.float32

    def dense(k, shape, fan_in):
        return jax.random.normal(k, shape, f32) * fan_in ** -0.5

    def gain(k, shape):
        return 1.0 + 0.02 * jax.random.normal(k, shape, f32)

    x = jax.random.normal(ks[0], (BATCH, SEQ, D_MODEL), f32)
    mem = jax.random.normal(ks[1], (BATCH, N_MEM, D_MODEL), f32)
    offsets = jax.random.randint(ks[2], (BATCH, 1), 0, 4096, dtype=jnp.int32)
    positions = offsets + jnp.arange(SEQ, dtype=jnp.int32)[None, :]
    return {
        'x': x,
        'mem': mem,
        'positions': positions,
        'norm_g': gain(ks[3], (DEPTH, N_NORMS, D_MODEL)),
        'ffn_w_up': dense(ks[4], (DEPTH, 2, D_MODEL, 2 * D_FF), D_MODEL),
        'ffn_w_down': dense(ks[5], (DEPTH, 2, D_FF, D_MODEL), D_FF),
        'w_in': dense(ks[6], (DEPTH, D_MODEL, IN_COLS), D_MODEL),
        'ssd_conv_w': dense(ks[7], (DEPTH, CONV_K, SSD_CONV_DIM), CONV_K),
        'ssd_conv_b': 0.02 * jax.random.normal(ks[8], (DEPTH, SSD_CONV_DIM), f32),
        'ssd_dt_bias': _dt_bias(ks[9], (DEPTH, SSD_HEADS)),
        'ssd_a_log': jnp.log(jax.random.uniform(ks[10], (DEPTH, SSD_HEADS), f32, 1.0, 16.0)),
        'ssd_d': 1.0 + 0.1 * jax.random.normal(ks[11], (DEPTH, SSD_HEADS), f32),
        'ssd_norm_g': gain(ks[12], (DEPTH, SSD_WIDTH)),
        'mla_q_norm_g': gain(ks[13], (DEPTH, MLA_Q_LORA)),
        'mla_w_uq': dense(ks[14], (DEPTH, MLA_Q_LORA, MLA_HEADS * (MLA_NOPE + MLA_ROPE)), MLA_Q_LORA),
        'mla_kv_norm_g': gain(ks[15], (DEPTH, MLA_KV_LORA)),
        'mla_w_ukv': dense(ks[16], (DEPTH, MLA_KV_LORA, MLA_HEADS * (MLA_NOPE + MLA_V)), MLA_KV_LORA),
        'gdn_conv_w': dense(ks[17], (DEPTH, CONV_K, GDN_CONV_DIM), CONV_K),
        'gdn_dt_bias': _dt_bias(ks[18], (DEPTH, GDN_HEADS)),
        'gdn_a_log': jnp.log(jax.random.uniform(ks[19], (DEPTH, GDN_HEADS), f32, 1.0, 16.0)),
        'gdn_norm_g': gain(ks[20], (DEPTH, GDN_DV)),
        'w_out': dense(ks[21], (DEPTH, D_MIX, D_MODEL), D_MIX),
        'xa_w_q': dense(ks[22], (DEPTH, D_MODEL, D_MODEL), D_MODEL),
        'xa_w_kv': dense(ks[23], (DEPTH, D_MODEL, 2 * D_MODEL), D_MODEL),
        'xa_w_o': dense(ks[24], (DEPTH, D_MODEL, D_MODEL), D_MODEL),
    }


def reference(x, mem, positions, norm_g, ffn_w_up, ffn_w_down, w_in, ssd_conv_w, ssd_conv_b,
              ssd_dt_bias, ssd_a_log, ssd_d, ssd_norm_g, mla_q_norm_g, mla_w_uq, mla_kv_norm_g,
              mla_w_ukv, gdn_conv_w, gdn_dt_bias, gdn_a_log, gdn_norm_g, w_out, xa_w_q, xa_w_kv,
              xa_w_o):
    cos, sin = rope_tables(positions)
    h = x
    for i in range(DEPTH):
        g = norm_g[i]
        ff = swiglu_ffn(rms_norm(h, g[FFN1_PRE]), ffn_w_up[i, 0], ffn_w_down[i, 0])
        h = h + 0.5 * rms_norm(ff, g[FFN1_POST])
        u = rms_norm(h, g[MIX_PRE])
        cols = u @ w_in[i]
        c_ssd, c_mla, c_gdn = jnp.split(cols, [SSD_IN, SSD_IN + MLA_IN], axis=-1)
        y_ssd = ssd_mixer(c_ssd, ssd_conv_w[i], ssd_conv_b[i], ssd_dt_bias[i], ssd_a_log[i],
                          ssd_d[i], ssd_norm_g[i])
        y_mla = mla_mixer(c_mla, cos, sin, mla_q_norm_g[i], mla_w_uq[i], mla_kv_norm_g[i], mla_w_ukv[i])
        y_gdn = gdn_mixer(c_gdn, gdn_conv_w[i], gdn_dt_bias[i], gdn_a_log[i], gdn_norm_g[i])
        y = jnp.concatenate([y_ssd, y_mla, y_gdn], axis=-1)
        h = h + rms_norm(y @ w_out[i], g[MIX_POST])
        xa = memory_cross_attention(rms_norm(h, g[XA_PRE]), rms_norm(mem, g[MEM_NORM]),
                                    xa_w_q[i], xa_w_kv[i], xa_w_o[i])
        h = h + rms_norm(xa, g[XA_POST])
        ff = swiglu_ffn(rms_norm(h, g[FFN2_PRE]), ffn_w_up[i, 1], ffn_w_down[i, 1])
        h = h + 0.5 * rms_norm(ff, g[FFN2_POST])
    return h
```

```python
import functools

import numpy as np
import jax
import jax.numpy as jnp
from jax import lax
from jax.experimental import pallas as pl
from jax.experimental.pallas import tpu as pltpu

F32 = jnp.float32
BF16 = jnp.bfloat16
HIGHEST = lax.Precision.HIGHEST

D_MODEL = 1024
DEPTH = 4
CHUNK = 64
NORM_EPS = 1e-6
CONV_K = 4
D_FF = 2816
N_MEM = 256

SSD_HEADS = 8
SSD_HEAD_DIM = 64
SSD_WIDTH = 512
SSD_GROUPS = 2
SSD_STATE = 64
SSD_CONV_DIM = 768
SSD_IN = 1288

MLA_HEADS = 4
MLA_Q_LORA = 256
MLA_KV_LORA = 128
MLA_NOPE = 64
MLA_ROPE = 32
MLA_V = 64
MLA_IN = 416
ROPE_THETA = 10000.0
MLA_SCALE = float((MLA_NOPE + MLA_ROPE) ** -0.5)

GDN_HEADS = 4
GDN_DK = 64
GDN_DV = 64
GDN_CONV_DIM = 768
GDN_WIDTH = 256
GDN_IN = 1032

XA_HEADS = 4
XA_HEAD_DIM = 256

(FFN1_PRE, FFN1_POST, MIX_PRE, MIX_POST, MEM_NORM, XA_PRE, XA_POST, FFN2_PRE, FFN2_POST) = range(9)

LANES = 128
HEAD_PAD = 128
NEG_BIG = -1e30
VMEM_LIMIT = 56 * 1024 * 1024

IN_COLS_PADDED = 3072
SSD_MAIN = 1280
MLA_MAIN = 640
GDN_MAIN = 1024
SMALL_DT, SMALL_B, SMALL_A = 0, 8, 12


def _in_perm():
    perm = -np.ones((IN_COLS_PADDED,), np.int64)
    perm[0:1280] = np.arange(0, 1280)
    mla0 = SSD_IN
    perm[1280:1664] = np.arange(mla0, mla0 + 384)
    kr0 = mla0 + 384
    half = MLA_ROPE // 2
    perm[1664 + 64:1664 + 96] = np.arange(kr0, kr0 + 32)
    perm[1792 + 64:1792 + 64 + half] = np.arange(kr0 + half, kr0 + 32)
    perm[1792 + 64 + half:1792 + 96] = np.arange(kr0, kr0 + half)
    gdn0 = SSD_IN + MLA_IN
    perm[1920:2944] = np.arange(gdn0, gdn0 + 1024)
    perm[2944 + SMALL_DT:2944 + SMALL_DT + 8] = np.arange(1280, 1288)
    perm[2944 + SMALL_B:2944 + SMALL_B + 4] = np.arange(gdn0 + 1024, gdn0 + 1028)
    perm[2944 + SMALL_A:2944 + SMALL_A + 4] = np.arange(gdn0 + 1028, gdn0 + 1032)
    return perm


_IN_PERM = _in_perm()


def _gather_cols(w, perm):
    valid = jnp.asarray(perm >= 0)
    idx = jnp.asarray(np.maximum(perm, 0))
    return jnp.where(valid, jnp.take(w, idx, axis=-1), 0.0)


def _uq_perms():
    a = -np.ones((MLA_HEADS * HEAD_PAD,), np.int64)
    b = -np.ones((MLA_HEADS * HEAD_PAD,), np.int64)
    hd = MLA_NOPE + MLA_ROPE
    half = MLA_ROPE // 2
    for h in range(MLA_HEADS):
        a[h * HEAD_PAD:h * HEAD_PAD + hd] = np.arange(h * hd, (h + 1) * hd)
        r0 = h * hd + MLA_NOPE
        b[h * HEAD_PAD + 64:h * HEAD_PAD + 64 + half] = np.arange(r0 + half, r0 + 32)
        b[h * HEAD_PAD + 64 + half:h * HEAD_PAD + 96] = np.arange(r0, r0 + half)
    return np.concatenate([a, b])


def _ukv_perms():
    k = -np.ones((MLA_HEADS * HEAD_PAD,), np.int64)
    v = -np.ones((MLA_HEADS * HEAD_PAD,), np.int64)
    hd = MLA_NOPE + MLA_V
    for h in range(MLA_HEADS):
        k[h * HEAD_PAD:h * HEAD_PAD + MLA_NOPE] = np.arange(h * hd, h * hd + MLA_NOPE)
        v[h * HEAD_PAD:h * HEAD_PAD + MLA_V] = np.arange(h * hd + MLA_NOPE, (h + 1) * hd)
    return np.concatenate([k, v])


_UQ_PERM = _uq_perms()
_UKV_PERM = _ukv_perms()


def _wout_mla_rows():
    rows = -np.ones((MLA_HEADS * HEAD_PAD,), np.int64)
    for h in range(MLA_HEADS):
        rows[h * HEAD_PAD:h * HEAD_PAD + MLA_V] = SSD_WIDTH + np.arange(h * MLA_V, (h + 1) * MLA_V)
    return rows


_WOUT_MLA_ROWS = _wout_mla_rows()


def _rms(x, g):
    return x * lax.rsqrt(jnp.mean(x * x, axis=-1, keepdims=True) + NORM_EPS) * g


def _silu(x):
    return x * jax.nn.sigmoid(x)


def _softplus(x):
    return jnp.maximum(x, 0.0) + jnp.log1p(jnp.exp(-jnp.abs(x)))


def _dot(a, b):
    return jnp.dot(a, b, preferred_element_type=F32)


def _dot_exact(a, b):
    return jnp.dot(a, b, preferred_element_type=F32, precision=HIGHEST)


def _iota(shape, dim):
    return lax.broadcasted_iota(jnp.int32, shape, dim)


def _params(sem):
    return pltpu.CompilerParams(dimension_semantics=sem, vmem_limit_bytes=VMEM_LIMIT)


def _ffn_body(h_ref, gpre_ref, wgu_ref, wd_ref, gpost_ref, o_ref, xn_ref, acc_ref, *, tf, nj):
    j = pl.program_id(1)

    @pl.when(j == 0)
    def _():
        xn_ref[...] = _rms(h_ref[...], gpre_ref[...]).astype(BF16)

    gu = _dot(xn_ref[...], wgu_ref[0])
    act = (_silu(gu[:, :tf]) * gu[:, tf:]).astype(BF16)
    part = _dot(act, wd_ref[0])

    @pl.when(j == 0)
    def _():
        acc_ref[...] = part

    @pl.when(j > 0)
    def _():
        acc_ref[...] += part

    @pl.when(j == nj - 1)
    def _():
        o_ref[...] = h_ref[...] + 0.5 * _rms(acc_ref[...], gpost_ref[...])


def _ffn(h, g_pre, wgu, wd, g_post, tm):
    n = h.shape[0]
    nj, _, tf2 = wgu.shape
    tf = tf2 // 2
    tm = min(tm, n)
    return pl.pallas_call(
        functools.partial(_ffn_body, tf=tf, nj=nj),
        grid=(n // tm, nj),
        in_specs=[
            pl.BlockSpec((tm, D_MODEL), lambda i, j: (i, 0)),
            pl.BlockSpec((1, D_MODEL), lambda i, j: (0, 0)),
            pl.BlockSpec((1, D_MODEL, tf2), lambda i, j: (j, 0, 0)),
            pl.BlockSpec((1, tf, D_MODEL), lambda i, j: (j, 0, 0)),
            pl.BlockSpec((1, D_MODEL), lambda i, j: (0, 0)),
        ],
        out_specs=pl.BlockSpec((tm, D_MODEL), lambda i, j: (i, 0)),
        out_shape=jax.ShapeDtypeStruct((n, D_MODEL), F32),
        scratch_shapes=[pltpu.VMEM((tm, D_MODEL), BF16), pltpu.VMEM((tm, D_MODEL), F32)],
        compiler_params=_params(("parallel", "arbitrary")),
        name="ffn",
    )(h, g_pre, wgu, wd, g_post)


def _inproj_body(h_ref, g_ref, w_ref, ssd_ref, mla_ref, gdn_ref, small_ref):
    xn = _rms(h_ref[...], g_ref[...]).astype(BF16)
    y = _dot(xn, w_ref[...])
    ssd_ref[...] = y[:, 0:1280]
    mla_ref[...] = y[:, 1280:1920]
    gdn_ref[...] = y[:, 1920:2944]
    small_ref[...] = y[:, 2944:3072]


def _inproj(h, g, w, tm):
    n = h.shape[0]
    tm = min(tm, n)
    row = lambda i: (i, 0)
    fixed = lambda i: (0, 0)
    return pl.pallas_call(
        _inproj_body,
        grid=(n // tm,),
        in_specs=[
            pl.BlockSpec((tm, D_MODEL), row),
            pl.BlockSpec((1, D_MODEL), fixed),
            pl.BlockSpec((D_MODEL, IN_COLS_PADDED), fixed),
        ],
        out_specs=[
            pl.BlockSpec((tm, SSD_MAIN), row),
            pl.BlockSpec((tm, MLA_MAIN), row),
            pl.BlockSpec((tm, GDN_MAIN), row),
            pl.BlockSpec((tm, LANES), row),
        ],
        out_shape=[
            jax.ShapeDtypeStruct((n, SSD_MAIN), F32),
            jax.ShapeDtypeStruct((n, MLA_MAIN), F32),
            jax.ShapeDtypeStruct((n, GDN_MAIN), F32),
            jax.ShapeDtypeStruct((n, LANES), F32),
        ],
        compiler_params=_params(("parallel",)),
        name="inproj",
    )(h, g, w)


def _causal_conv(x, xpad_ref, cw_ref, tile):
    xpad_ref[8:8 + tile, :] = x
    acc = cw_ref[CONV_K - 1:CONV_K, :] * x
    for k in range(CONV_K - 1):
        off = 8 - (CONV_K - 1) + k
        acc = acc + cw_ref[k:k + 1, :] * xpad_ref[off:off + tile, :]
    xpad_ref[0:8, :] = x[tile - 8:tile, :]
    return acc


def _ssd_body(main_ref, small_ref, cw_ref, cb_ref, dtb_ref, alog_ref, dsk_ref, ng_ref,
              y_ref, xpad_ref, st_ref, *, L):
    t = pl.program_id(1)

    @pl.when(t == 0)
    def _():
        xpad_ref[0:8, :] = jnp.zeros((8, SSD_CONV_DIM), F32)
        st_ref[...] = jnp.zeros_like(st_ref)

    z = main_ref[:, 0:SSD_WIDTH]
    xbc = main_ref[:, SSD_WIDTH:SSD_MAIN]
    xa = _silu(_causal_conv(xbc, xpad_ref, cw_ref, L) + cb_ref[...])
    xs = xa[:, 0:512]
    bm = xa[:, 512:640]
    cm = xa[:, 640:768]

    dt = _softplus(small_ref[...] + dtb_ref[...])
    dta = dt * (-jnp.exp(alog_ref[...]))
    ri = _iota((L, L), 0)
    ci = _iota((L, L), 1)
    causal = ri >= ci
    acum = _dot_exact(causal.astype(F32), dta)
    expand = (_iota((LANES, SSD_WIDTH), 0) == _iota((LANES, SSD_WIDTH), 1) // SSD_HEAD_DIM).astype(F32)
    dt_x = _dot_exact(dt, expand)
    acum_x = _dot_exact(acum, expand)
    acum_t = acum.T
    bm_t = bm.T.astype(BF16)
    alast_x = acum_x[L - 1:L, :]
    xdt = xs * dt_x

    state_in = st_ref[...]
    y_off = _dot(cm.astype(BF16), state_in.astype(BF16)) * jnp.exp(acum_x)
    w_end = (xdt * jnp.exp(alast_x - acum_x)).astype(BF16)
    chunk_state = _dot(bm_t, w_end)
    own_group = (_iota((LANES, SSD_WIDTH), 0) // SSD_STATE) == (_iota((LANES, SSD_WIDTH), 1) // 256)
    st_ref[...] = jnp.where(own_group, state_in * jnp.exp(alast_x) + chunk_state, 0.0)

    lane = _iota((L, LANES), 1)
    group_of_lane = _iota((1, LANES), 1) // SSD_STATE
    parts = []
    for g in range(SSD_GROUPS):
        cb = _dot(jnp.where(group_of_lane == g, cm, 0.0).astype(BF16), bm_t)
        for pair in range(2):
            slot = g * 2 + pair
            xp = xdt[:, slot * LANES:(slot + 1) * LANES].astype(BF16)
            res = []
            for k in range(2):
                hh = slot * 2 + k
                seg = jnp.where(causal, acum[:, hh:hh + 1] - acum_t[hh:hh + 1, :], NEG_BIG)
                res.append(_dot((cb * jnp.exp(seg)).astype(BF16), xp))
            parts.append(jnp.where(lane < SSD_HEAD_DIM, res[0], res[1]))
    y = jnp.concatenate(parts, axis=1) + y_off + xs * dsk_ref[...]
    y = y * _silu(z)
    outs = []
    for g in range(SSD_GROUPS):
        yg = y[:, g * 256:(g + 1) * 256]
        outs.append(yg * lax.rsqrt(jnp.mean(yg * yg, axis=-1, keepdims=True) + NORM_EPS))
    y_ref[...] = (jnp.concatenate(outs, axis=1) * ng_ref[...]).astype(BF16)


def _ssd(main, small, cw, cb, dtb, alog, dsk, ng, bsz, seq, L):
    L = min(L, seq)
    nt = seq // L
    row = lambda b, t: (b * nt + t, 0)
    fixed = lambda b, t: (0, 0)
    return pl.pallas_call(
        functools.partial(_ssd_body, L=L),
        grid=(bsz, nt),
        in_specs=[
            pl.BlockSpec((L, SSD_MAIN), row),
            pl.BlockSpec((L, LANES), row),
            pl.BlockSpec((CONV_K, SSD_CONV_DIM), fixed),
            pl.BlockSpec((1, SSD_CONV_DIM), fixed),
            pl.BlockSpec((1, LANES), fixed),
            pl.BlockSpec((1, LANES), fixed),
            pl.BlockSpec((1, SSD_WIDTH), fixed),
            pl.BlockSpec((1, SSD_WIDTH), fixed),
        ],
        out_specs=pl.BlockSpec((L, SSD_WIDTH), row),
        out_shape=jax.ShapeDtypeStruct((bsz * seq, SSD_WIDTH), BF16),
        scratch_shapes=[pltpu.VMEM((8 + L, SSD_CONV_DIM), F32), pltpu.VMEM((LANES, SSD_WIDTH), F32)],
        compiler_params=_params(("parallel", "arbitrary")),
        name="ssd",
    )(main, small, cw, cb, dtb, alog, dsk, ng)


def _gdn_body(main_ref, small_ref, cw_ref, dtb_ref, alog_ref, ng_ref,
              y_ref, xpad_ref, s_ref, vnew_ref, *, T):
    t = pl.program_id(1)

    @pl.when(t == 0)
    def _():
        xpad_ref[0:8, :] = jnp.zeros((8, GDN_CONV_DIM), F32)
        s_ref[...] = jnp.zeros_like(s_ref)

    qkv_raw = main_ref[:, 0:GDN_CONV_DIM]
    z = main_ref[:, GDN_CONV_DIM:GDN_MAIN]
    xa = _silu(_causal_conv(qkv_raw, xpad_ref, cw_ref, T))
    q = xa[:, 0:256]
    k = xa[:, 256:512]
    v = xa[:, 512:768]

    same_head = (_iota((256, 256), 0) // GDN_DK) == (_iota((256, 256), 1) // GDN_DK)
    head_ones = same_head.astype(F32)
    qn = q * lax.rsqrt(_dot_exact(q * q, head_ones) + NORM_EPS) * (GDN_DK ** -0.5)
    kn = k * lax.rsqrt(_dot_exact(k * k, head_ones) + NORM_EPS)

    sm = small_ref[...]
    beta = jax.nn.sigmoid(sm)
    gl = -jnp.exp(alog_ref[...]) * _softplus(sm + dtb_ref[...])
    ri = _iota((T, T), 0)
    ci = _iota((T, T), 1)
    same_chunk = (ri // CHUNK) == (ci // CHUNK)
    incl = jnp.logical_and(same_chunk, ri >= ci)
    strict = jnp.logical_and(same_chunk, ri > ci)
    gcum = _dot_exact(incl.astype(F32), gl)
    er = _iota((LANES, GDN_WIDTH), 0)
    ec = _iota((LANES, GDN_WIDTH), 1) // GDN_DK
    beta_x = _dot_exact(beta, (er == ec + SMALL_B).astype(F32))
    gcum_x = _dot_exact(gcum, (er == ec + SMALL_A).astype(F32))
    last_sel = (ci == (ri // CHUNK) * CHUNK + (CHUNK - 1)).astype(F32)
    glast_x = _dot_exact(last_sel, gcum_x)
    gcum_t = gcum.T

    kb = kn * beta_x
    eg = jnp.exp(gcum_x)
    rhs = jnp.concatenate([v * beta_x, kb * eg], axis=1).astype(BF16)
    q_dec = (qn * eg).astype(BF16)
    k_dec_t = (kn * jnp.exp(glast_x - gcum_x)).T
    kn_t = kn.T.astype(BF16)
    head_of_lane = _iota((1, GDN_WIDTH), 1) // GDN_DK
    eye = (ri == ci).astype(F32)
    lower_left = []
    b = 1
    while b < CHUNK:
        in_block = (ri // (2 * b)) == (ci // (2 * b))
        lower_left.append(jnp.logical_and(in_block, (ri // b) % 2 > (ci // b) % 2))
        b *= 2

    u = jnp.zeros((T, GDN_WIDTH), F32)
    w = jnp.zeros((T, GDN_WIDTH), F32)
    qk_heads = []
    for h in range(GDN_HEADS):
        hm = head_of_lane == h
        col = gcum[:, SMALL_A + h:SMALL_A + h + 1]
        rowv = gcum_t[SMALL_A + h:SMALL_A + h + 1, :]
        gamma = jnp.exp(jnp.where(incl, col - rowv, NEG_BIG))
        kk = _dot(jnp.where(hm, kb, 0.0).astype(BF16), kn_t)
        mlow = jnp.where(strict, kk * gamma, 0.0)
        p = eye - jnp.where(lower_left[0], mlow, 0.0)
        for ll in lower_left[1:]:
            pb = p.astype(BF16)
            p = p - _dot(_dot(pb, jnp.where(ll, mlow, 0.0).astype(BF16)).astype(BF16), pb)
        sol = _dot(p.astype(BF16), rhs)
        u = jnp.where(hm, sol[:, 0:256], u)
        w = jnp.where(hm, sol[:, 256:512], w)
        qk = _dot(jnp.where(hm, qn, 0.0).astype(BF16), kn_t)
        qk_heads.append(jnp.where(incl, qk * gamma, 0.0).astype(BF16))

    vnew_ref[...] = jnp.zeros((T, GDN_WIDTH), F32)
    chunk_of_col = _iota((1, T), 1) // CHUNK
    o_inter = []
    for c in range(T // CHUNK):
        r0 = c * CHUNK
        s_in = s_ref[...]
        s_b = s_in.astype(BF16)
        vnew_ref[r0:r0 + CHUNK, :] = u[r0:r0 + CHUNK, :] - _dot(w[r0:r0 + CHUNK, :].astype(BF16), s_b)
        o_inter.append(_dot(q_dec[r0:r0 + CHUNK, :], s_b))
        kd = jnp.where(chunk_of_col == c, k_dec_t, 0.0).astype(BF16)
        upd = _dot(kd, vnew_ref[...].astype(BF16))
        dec = jnp.exp(glast_x[r0:r0 + 1, :])
        s_ref[...] = jnp.where(same_head, s_in * dec + upd, 0.0)

    vnew = vnew_ref[...].astype(BF16)
    o = jnp.concatenate(o_inter, axis=0)
    for h in range(GDN_HEADS):
        o = o + jnp.where(head_of_lane == h, _dot(qk_heads[h], vnew), 0.0)
    ms = _dot_exact(o * o, head_ones) * (1.0 / GDN_DV)
    y = o * lax.rsqrt(ms + NORM_EPS) * ng_ref[...] * _silu(z)
    y_ref[...] = y.astype(BF16)


def _gdn(main, small, cw, dtb, alog, ng, bsz, seq, T):
    T = min(T, seq)
    nt = seq // T
    row = lambda b, t: (b * nt + t, 0)
    fixed = lambda b, t: (0, 0)
    return pl.pallas_call(
        functools.partial(_gdn_body, T=T),
        grid=(bsz, nt),
        in_specs=[
            pl.BlockSpec((T, GDN_MAIN), row),
            pl.BlockSpec((T, LANES), row),
            pl.BlockSpec((CONV_K, GDN_CONV_DIM), fixed),
            pl.BlockSpec((1, LANES), fixed),
            pl.BlockSpec((1, LANES), fixed),
            pl.BlockSpec((1, GDN_WIDTH), fixed),
        ],
        out_specs=pl.BlockSpec((T, GDN_WIDTH), row),
        out_shape=jax.ShapeDtypeStruct((bsz * seq, GDN_WIDTH), BF16),
        scratch_shapes=[
            pltpu.VMEM((8 + T, GDN_CONV_DIM), F32),
            pltpu.VMEM((GDN_WIDTH, GDN_WIDTH), F32),
            pltpu.VMEM((T, GDN_WIDTH), F32),
        ],
        compiler_params=_params(("parallel", "arbitrary")),
        name="gdn",
    )(main, small, cw, dtb, alog, ng)


def _mla_prep_body(c_ref, cos_ref, sin_ref, gq_ref, gkv_ref, wq_ref, wkv_ref, vone_ref,
                   q_ref, kt_ref, v_ref):
    c = c_ref[...]
    cq = _rms(c[:, 0:256], gq_ref[...]).astype(BF16)
    ckv = _rms(c[:, 256:384], gkv_ref[...]).astype(BF16)
    cos_t = cos_ref[...]
    sin_t = sin_ref[...]
    k_rot = c[:, 384:512] * cos_t + c[:, 512:640] * sin_t
    qq = _dot(cq, wq_ref[...])
    kvv = _dot(ckv, wkv_ref[...])
    width = MLA_HEADS * HEAD_PAD
    qs, ks = [], []
    for h in range(MLA_HEADS):
        lo, hi = h * HEAD_PAD, (h + 1) * HEAD_PAD
        qs.append((qq[:, lo:hi] * cos_t + qq[:, width + lo:width + hi] * sin_t) * MLA_SCALE)
        ks.append(kvv[:, lo:hi] + k_rot)
    q_ref[...] = jnp.concatenate(qs, axis=1).astype(BF16)
    kt_ref[0, 0] = jnp.concatenate(ks, axis=1).T.astype(BF16)
    v_ref[...] = (kvv[:, width:2 * width] + vone_ref[...]).astype(BF16)


def _mla_prep(cols, cos_t, sin_t, gq, gkv, wq, wkv, vone, bsz, seq, tm):
    tm = min(tm, seq)
    nt = seq // tm
    width = MLA_HEADS * HEAD_PAD
    row = lambda i: (i, 0)
    fixed = lambda i: (0, 0)
    return pl.pallas_call(
        _mla_prep_body,
        grid=(bsz * nt,),
        in_specs=[
            pl.BlockSpec((tm, MLA_MAIN), row),
            pl.BlockSpec((tm, LANES), row),
            pl.BlockSpec((tm, LANES), row),
            pl.BlockSpec((1, MLA_Q_LORA), fixed),
            pl.BlockSpec((1, MLA_KV_LORA), fixed),
            pl.BlockSpec((MLA_Q_LORA, 2 * width), fixed),
            pl.BlockSpec((MLA_KV_LORA, 2 * width), fixed),
            pl.BlockSpec((1, width), fixed),
        ],
        out_specs=[
            pl.BlockSpec((tm, width), row),
            pl.BlockSpec((1, 1, width, tm), lambda i: (i // nt, i % nt, 0, 0)),
            pl.BlockSpec((tm, width), row),
        ],
        out_shape=[
            jax.ShapeDtypeStruct((bsz * seq, width), BF16),
            jax.ShapeDtypeStruct((bsz, nt, width, tm), BF16),
            jax.ShapeDtypeStruct((bsz * seq, width), BF16),
        ],
        compiler_params=_params(("parallel",)),
        name="mla_prep",
    )(cols, cos_t, sin_t, gq, gkv, wq, wkv, vone)


def _mla_attn_body(q_ref, kt_ref, v_ref, o_ref, *, tq):
    i = pl.program_id(1)
    visible = (_iota((tq, tq), 1) // CHUNK) <= (_iota((tq, tq), 0) // CHUNK)
    outs = []
    for h in range(MLA_HEADS):
        lo, hi = h * HEAD_PAD, (h + 1) * HEAD_PAD
        q = q_ref[:, lo:hi]

        def step(j, carry, diagonal, lo=lo, hi=hi, q=q):
            m, acc = carry
            s = _dot(q, kt_ref[0, j, lo:hi, :])
            if diagonal:
                s = jnp.where(visible, s, NEG_BIG)
            m_new = jnp.maximum(m, jnp.max(s, axis=-1, keepdims=True))
            p = jnp.exp(s - m_new).astype(BF16)
            vblk = v_ref[0, pl.ds(pl.multiple_of(j * tq, tq), tq), lo:hi]
            return m_new, jnp.exp(m - m_new) * acc + _dot(p, vblk)

        carry = (jnp.full((tq, 1), NEG_BIG, F32), jnp.zeros((tq, HEAD_PAD), F32))
        carry = lax.fori_loop(0, i, functools.partial(step, diagonal=False), carry)
        _, acc = step(i, carry, True)
        outs.append(acc / acc[:, MLA_V:MLA_V + 1])
    o_ref[...] = jnp.concatenate(outs, axis=1).astype(BF16)


def _mla_attn(q, kt, v, bsz, seq, tq):
    tq = min(tq, seq)
    nq = seq // tq
    width = MLA_HEADS * HEAD_PAD
    return pl.pallas_call(
        functools.partial(_mla_attn_body, tq=tq),
        grid=(bsz, nq),
        in_specs=[
            pl.BlockSpec((tq, width), lambda b, i: (b * nq + i, 0)),
            pl.BlockSpec((1, nq, width, tq), lambda b, i: (b, 0, 0, 0)),
            pl.BlockSpec((1, seq, width), lambda b, i: (b, 0, 0)),
        ],
        out_specs=pl.BlockSpec((tq, width), lambda b, i: (b * nq + i, 0)),
        out_shape=jax.ShapeDtypeStruct((bsz * seq, width), BF16),
        compiler_params=_params(("parallel", "arbitrary")),
        name="mla_attn",
    )(q, kt, v.reshape(bsz, seq, width))


def _memkv_body(mem_ref, g_ref, w_ref, kt_ref, v_ref):
    mn = _rms(mem_ref[...], g_ref[...]).astype(BF16)
    kv = _dot(mn, w_ref[...])
    kt_ref[0] = kv[:, 0:D_MODEL].T.astype(BF16)
    v_ref[0] = kv[:, D_MODEL:2 * D_MODEL].astype(BF16)


def _memkv(mem2d, g, w, bsz):
    return pl.pallas_call(
        _memkv_body,
        grid=(bsz,),
        in_specs=[
            pl.BlockSpec((N_MEM, D_MODEL), lambda b: (b, 0)),
            pl.BlockSpec((1, D_MODEL), lambda b: (0, 0)),
            pl.BlockSpec((D_MODEL, 2 * D_MODEL), lambda b: (0, 0)),
        ],
        out_specs=[
            pl.BlockSpec((1, D_MODEL, N_MEM), lambda b: (b, 0, 0)),
            pl.BlockSpec((1, N_MEM, D_MODEL), lambda b: (b, 0, 0)),
        ],
        out_shape=[
            jax.ShapeDtypeStruct((bsz, D_MODEL, N_MEM), BF16),
            jax.ShapeDtypeStruct((bsz, N_MEM, D_MODEL), BF16),
        ],
        compiler_params=_params(("parallel",)),
        name="memkv",
    )(mem2d, g, w)


def _post_body(h_ref, ys_ref, ym_ref, yg_ref, wos_ref, wom_ref, wog_ref, gmix_ref, gxa_ref,
               wq_ref, kt_ref, v_ref, wo_ref, gpost_ref, o_ref):
    mix = _dot(ys_ref[...], wos_ref[...]) + _dot(ym_ref[...], wom_ref[...]) + _dot(yg_ref[...], wog_ref[...])
    h1 = h_ref[...] + _rms(mix, gmix_ref[...])
    u = _rms(h1, gxa_ref[...]).astype(BF16)
    q = (_dot(u, wq_ref[...]) * (XA_HEAD_DIM ** -0.5)).astype(BF16)
    outs = []
    for hd in range(XA_HEADS):
        lo, hi = hd * XA_HEAD_DIM, (hd + 1) * XA_HEAD_DIM
        s = _dot(q[:, lo:hi], kt_ref[0, lo:hi, :])
        p = jnp.exp(s - jnp.max(s, axis=-1, keepdims=True))
        o = _dot(p.astype(BF16), v_ref[0, :, lo:hi])
        outs.append((o / jnp.sum(p, axis=-1, keepdims=True)).astype(BF16))
    xa = _dot(jnp.concatenate(outs, axis=1), wo_ref[...])
    o_ref[...] = h1 + _rms(xa, gpost_ref[...])


def _post(h, ys, ym, yg, wos, wom, wog, gmix, gxa, wq, kt, v, wo, gpost, bsz, seq, tm):
    tm = min(tm, seq)
    nt = seq // tm
    row = lambda b, t: (b * nt + t, 0)
    fixed = lambda b, t: (0, 0)
    full = lambda a: pl.BlockSpec(a.shape, fixed)
    return pl.pallas_call(
        _post_body,
        grid=(bsz, nt),
        in_specs=[
            pl.BlockSpec((tm, D_MODEL), row),
            pl.BlockSpec((tm, ys.shape[1]), row),
            pl.BlockSpec((tm, ym.shape[1]), row),
            pl.BlockSpec((tm, yg.shape[1]), row),
            full(wos), full(wom), full(wog), full(gmix), full(gxa), full(wq),
            pl.BlockSpec((1, D_MODEL, N_MEM), lambda b, t: (b, 0, 0)),
            pl.BlockSpec((1, N_MEM, D_MODEL), lambda b, t: (b, 0, 0)),
            full(wo), full(gpost),
        ],
        out_specs=pl.BlockSpec((tm, D_MODEL), row),
        out_shape=jax.ShapeDtypeStruct((bsz * seq, D_MODEL), F32),
        compiler_params=_params(("parallel", "arbitrary")),
        name="mix_out_xattn",
    )(h, ys, ym, yg, wos, wom, wog, gmix, gxa, wq, kt, v, wo, gpost)


def _pad_lanes(vals, offset, width=LANES):
    n = vals.shape[-1]
    pad = [(0, 0)] * (vals.ndim - 1) + [(offset, width - offset - n)]
    return jnp.pad(vals, pad)[..., None, :]


def _prep_params(norm_g, ffn_w_up, ffn_w_down, w_in, ssd_conv_w, ssd_conv_b, ssd_dt_bias, ssd_a_log,
                 ssd_d, ssd_norm_g, mla_q_norm_g, mla_w_uq, mla_kv_norm_g, mla_w_ukv, gdn_conv_w,
                 gdn_dt_bias, gdn_a_log, gdn_norm_g, w_out, xa_w_q, xa_w_kv, xa_w_o, tf):
    nj = D_FF // tf
    depth = norm_g.shape[0]
    gate = ffn_w_up[..., :D_FF].reshape(depth, 2, D_MODEL, nj, 1, tf)
    up = ffn_w_up[..., D_FF:].reshape(depth, 2, D_MODEL, nj, 1, tf)
    wgu = jnp.concatenate([gate, up], axis=4).astype(BF16)
    wgu = jnp.transpose(wgu, (0, 1, 3, 2, 4, 5)).reshape(depth, 2, nj, D_MODEL, 2 * tf)
    vone = np.zeros((1, MLA_HEADS * HEAD_PAD), np.float32)
    vone[0, np.arange(MLA_HEADS) * HEAD_PAD + MLA_V] = 1.0
    return dict(
        norm_g=norm_g[:, :, None, :],
        wgu=wgu,
        wd=ffn_w_down.astype(BF16).reshape(depth, 2, nj, tf, D_MODEL),
        w_in=_gather_cols(w_in, _IN_PERM).astype(BF16),
        ssd_cw=ssd_conv_w,
        ssd_cb=ssd_conv_b[:, None, :],
        ssd_dtb=_pad_lanes(ssd_dt_bias, SMALL_DT),
        ssd_alog=_pad_lanes(ssd_a_log, SMALL_DT),
        ssd_dsk=jnp.repeat(ssd_d, SSD_HEAD_DIM, axis=-1)[:, None, :],
        ssd_ng=ssd_norm_g[:, None, :],
        mla_gq=mla_q_norm_g[:, None, :],
        mla_gkv=mla_kv_norm_g[:, None, :],
        mla_wq=_gather_cols(mla_w_uq, _UQ_PERM).astype(BF16),
        mla_wkv=_gather_cols(mla_w_ukv, _UKV_PERM).astype(BF16),
        mla_vone=jnp.broadcast_to(jnp.asarray(vone), (depth,) + vone.shape),
        gdn_cw=gdn_conv_w,
        gdn_dtb=_pad_lanes(gdn_dt_bias, SMALL_A),
        gdn_alog=_pad_lanes(gdn_a_log, SMALL_A),
        gdn_ng=jnp.tile(gdn_norm_g, (1, GDN_HEADS))[:, None, :],
        wo_ssd=w_out[:, 0:SSD_WIDTH, :].astype(BF16),
        wo_mla=jnp.swapaxes(_gather_cols(jnp.swapaxes(w_out, 1, 2), _WOUT_MLA_ROWS), 1, 2).astype(BF16),
        wo_gdn=w_out[:, SSD_WIDTH + MLA_HEADS * MLA_V:, :].astype(BF16),
        xa_wq=xa_w_q.astype(BF16),
        xa_wkv=xa_w_kv.astype(BF16),
        xa_wo=xa_w_o.astype(BF16),
    )


def _rope_tables(positions):
    inv = 1.0 / (ROPE_THETA ** (jnp.arange(0, MLA_ROPE, 2, dtype=F32) / MLA_ROPE))
    ang = positions.astype(F32).reshape(-1, 1) * inv
    cos, sin = jnp.cos(ang), jnp.sin(ang)
    n = ang.shape[0]
    ones = jnp.ones((n, MLA_NOPE), F32)
    zeros_nope = jnp.zeros((n, MLA_NOPE), F32)
    zeros_pad = jnp.zeros((n, HEAD_PAD - MLA_NOPE - MLA_ROPE), F32)
    cos_t = jnp.concatenate([ones, cos, cos, zeros_pad], axis=1)
    sin_t = jnp.concatenate([zeros_nope, -sin, sin, zeros_pad], axis=1)
    return cos_t, sin_t


FFN_TM = 1024
FFN_TF = 256
TOKEN_TM = 512
SEQ_TILE = 256


def kernel(x, mem, positions, norm_g, ffn_w_up, ffn_w_down, w_in, ssd_conv_w, ssd_conv_b, ssd_dt_bias,
           ssd_a_log, ssd_d, ssd_norm_g, mla_q_norm_g, mla_w_uq, mla_kv_norm_g, mla_w_ukv, gdn_conv_w,
           gdn_dt_bias, gdn_a_log, gdn_norm_g, w_out, xa_w_q, xa_w_kv, xa_w_o):
    bsz, seq, _ = x.shape
    n = bsz * seq
    cos_t, sin_t = _rope_tables(positions)
    params = _prep_params(norm_g, ffn_w_up, ffn_w_down, w_in, ssd_conv_w, ssd_conv_b, ssd_dt_bias,
                          ssd_a_log, ssd_d, ssd_norm_g, mla_q_norm_g, mla_w_uq, mla_kv_norm_g, mla_w_ukv,
                          gdn_conv_w, gdn_dt_bias, gdn_a_log, gdn_norm_g, w_out, xa_w_q, xa_w_kv, xa_w_o,
                          FFN_TF)
    mem2d = mem.reshape(bsz * N_MEM, D_MODEL)

    def layer(h, p):
        g = p["norm_g"]
        h = _ffn(h, g[FFN1_PRE], p["wgu"][0], p["wd"][0], g[FFN1_POST], FFN_TM)
        c_ssd, c_mla, c_gdn, c_small = _inproj(h, g[MIX_PRE], p["w_in"], TOKEN_TM)
        y_ssd = _ssd(c_ssd, c_small, p["ssd_cw"], p["ssd_cb"], p["ssd_dtb"], p["ssd_alog"], p["ssd_dsk"],
                     p["ssd_ng"], bsz, seq, SEQ_TILE)
        q, kt, v = _mla_prep(c_mla, cos_t, sin_t, p["mla_gq"], p["mla_gkv"], p["mla_wq"], p["mla_wkv"],
                             p["mla_vone"], bsz, seq, SEQ_TILE)
        y_mla = _mla_attn(q, kt, v, bsz, seq, SEQ_TILE)
        y_gdn = _gdn(c_gdn, c_small, p["gdn_cw"], p["gdn_dtb"], p["gdn_alog"], p["gdn_ng"], bsz, seq, SEQ_TILE)
        mkt, mv = _memkv(mem2d, g[MEM_NORM], p["xa_wkv"], bsz)
        h = _post(h, y_ssd, y_mla, y_gdn, p["wo_ssd"], p["wo_mla"], p["wo_gdn"], g[MIX_POST], g[XA_PRE],
                  p["xa_wq"], mkt, mv, p["xa_wo"], g[XA_POST], bsz, seq, TOKEN_TM)
        h = _ffn(h, g[FFN2_PRE], p["wgu"][1], p["wd"][1], g[FFN2_POST], FFN_TM)
        return h, None

    h, _ = lax.scan(layer, x.reshape(n, D_MODEL), params)
    return h.reshape(bsz, seq, D_MODEL)
```

```python
import functools

import numpy as np
import jax
import jax.numpy as jnp
from jax import lax
from jax.experimental import pallas as pl
from jax.experimental.pallas import tpu as pltpu

F32 = jnp.float32
BF16 = jnp.bfloat16
HIGHEST = lax.Precision.HIGHEST

D_MODEL = 1024
DEPTH = 4
CHUNK = 64
NORM_EPS = 1e-6
CONV_K = 4
D_FF = 2816
N_MEM = 256

SSD_HEADS = 8
SSD_HEAD_DIM = 64
SSD_WIDTH = 512
SSD_GROUPS = 2
SSD_STATE = 64
SSD_CONV_DIM = 768
SSD_IN = 1288

MLA_HEADS = 4
MLA_Q_LORA = 256
MLA_KV_LORA = 128
MLA_NOPE = 64
MLA_ROPE = 32
MLA_V = 64
MLA_IN = 416
ROPE_THETA = 10000.0
MLA_SCALE = float((MLA_NOPE + MLA_ROPE) ** -0.5)
MLA_Q_SCALE = MLA_SCALE * float(np.log2(np.e))

GDN_HEADS = 4
GDN_DK = 64
GDN_DV = 64
GDN_CONV_DIM = 768
GDN_WIDTH = 256
GDN_IN = 1032

XA_HEADS = 4
XA_HEAD_DIM = 256

(FFN1_PRE, FFN1_POST, MIX_PRE, MIX_POST, MEM_NORM, XA_PRE, XA_POST, FFN2_PRE, FFN2_POST) = range(9)

LANES = 128
HEAD_PAD = 128
NEG_BIG = -1e30
VMEM_LIMIT = 56 * 1024 * 1024

IN_COLS_PADDED = 3072
SSD_MAIN = 1280
MLA_MAIN = 640
GDN_MAIN = 1024
SMALL_DT, SMALL_B, SMALL_A = 0, 8, 12


def _in_perm():
    perm = -np.ones((IN_COLS_PADDED,), np.int64)
    perm[0:1280] = np.arange(0, 1280)
    mla0 = SSD_IN
    perm[1280:1664] = np.arange(mla0, mla0 + 384)
    kr0 = mla0 + 384
    half = MLA_ROPE // 2
    perm[1664 + 64:1664 + 96] = np.arange(kr0, kr0 + 32)
    perm[1792 + 64:1792 + 64 + half] = np.arange(kr0 + half, kr0 + 32)
    perm[1792 + 64 + half:1792 + 96] = np.arange(kr0, kr0 + half)
    gdn0 = SSD_IN + MLA_IN
    perm[1920:2944] = np.arange(gdn0, gdn0 + 1024)
    perm[2944 + SMALL_DT:2944 + SMALL_DT + 8] = np.arange(1280, 1288)
    perm[2944 + SMALL_B:2944 + SMALL_B + 4] = np.arange(gdn0 + 1024, gdn0 + 1028)
    perm[2944 + SMALL_A:2944 + SMALL_A + 4] = np.arange(gdn0 + 1028, gdn0 + 1032)
    return perm


_IN_PERM = _in_perm()


def _gather_cols(w, perm):
    valid = jnp.asarray(perm >= 0)
    idx = jnp.asarray(np.maximum(perm, 0))
    return jnp.where(valid, jnp.take(w, idx, axis=-1), 0.0)


def _uq_perms():
    a = -np.ones((MLA_HEADS * HEAD_PAD,), np.int64)
    b = -np.ones((MLA_HEADS * HEAD_PAD,), np.int64)
    hd = MLA_NOPE + MLA_ROPE
    half = MLA_ROPE // 2
    for h in range(MLA_HEADS):
        a[h * HEAD_PAD:h * HEAD_PAD + hd] = np.arange(h * hd, (h + 1) * hd)
        r0 = h * hd + MLA_NOPE
        b[h * HEAD_PAD + 64:h * HEAD_PAD + 64 + half] = np.arange(r0 + half, r0 + 32)
        b[h * HEAD_PAD + 64 + half:h * HEAD_PAD + 96] = np.arange(r0, r0 + half)
    return np.concatenate([a, b])


def _ukv_perms():
    k = -np.ones((MLA_HEADS * HEAD_PAD,), np.int64)
    v = -np.ones((MLA_HEADS * HEAD_PAD,), np.int64)
    hd = MLA_NOPE + MLA_V
    for h in range(MLA_HEADS):
        k[h * HEAD_PAD:h * HEAD_PAD + MLA_NOPE] = np.arange(h * hd, h * hd + MLA_NOPE)
        v[h * HEAD_PAD:h * HEAD_PAD + MLA_V] = np.arange(h * hd + MLA_NOPE, (h + 1) * hd)
    return np.concatenate([k, v])


_UQ_PERM = _uq_perms()
_UKV_PERM = _ukv_perms()


def _wout_mla_rows():
    rows = -np.ones((MLA_HEADS * HEAD_PAD,), np.int64)
    for h in range(MLA_HEADS):
        rows[h * HEAD_PAD:h * HEAD_PAD + MLA_V] = SSD_WIDTH + np.arange(h * MLA_V, (h + 1) * MLA_V)
    return rows


_WOUT_MLA_ROWS = _wout_mla_rows()


def _rms(x, g):
    return x * lax.rsqrt(jnp.mean(x * x, axis=-1, keepdims=True) + NORM_EPS) * g


def _silu(x):
    return x * jax.nn.sigmoid(x)


def _softplus(x):
    return jnp.maximum(x, 0.0) + jnp.log1p(jnp.exp(-jnp.abs(x)))


def _dot(a, b):
    return jnp.dot(a, b, preferred_element_type=F32)


def _dot_exact(a, b):
    return jnp.dot(a, b, preferred_element_type=F32, precision=HIGHEST)


def _iota(shape, dim):
    return lax.broadcasted_iota(jnp.int32, shape, dim)


def _params(sem):
    return pltpu.CompilerParams(dimension_semantics=sem, vmem_limit_bytes=VMEM_LIMIT)


def _ffn_body(h_ref, gpre_ref, wgu_ref, wd_ref, gpost_ref, o_ref, xn_ref, acc_ref, *, tf, nj):
    j = pl.program_id(1)

    @pl.when(j == 0)
    def _():
        xn_ref[...] = _rms(h_ref[...], gpre_ref[...]).astype(BF16)

    gu = _dot(xn_ref[...], wgu_ref[0])
    act = (_silu(gu[:, :tf]) * gu[:, tf:]).astype(BF16)
    part = _dot(act, wd_ref[0])

    @pl.when(j == 0)
    def _():
        acc_ref[...] = part

    @pl.when(j > 0)
    def _():
        acc_ref[...] += part

    @pl.when(j == nj - 1)
    def _():
        o_ref[...] = h_ref[...] + 0.5 * _rms(acc_ref[...], gpost_ref[...])


def _ffn(h, g_pre, wgu, wd, g_post, tm):
    n = h.shape[0]
    nj, _, tf2 = wgu.shape
    tf = tf2 // 2
    tm = min(tm, n)
    return pl.pallas_call(
        functools.partial(_ffn_body, tf=tf, nj=nj),
        grid=(n // tm, nj),
        in_specs=[
            pl.BlockSpec((tm, D_MODEL), lambda i, j: (i, 0)),
            pl.BlockSpec((1, D_MODEL), lambda i, j: (0, 0)),
            pl.BlockSpec((1, D_MODEL, tf2), lambda i, j: (j, 0, 0)),
            pl.BlockSpec((1, tf, D_MODEL), lambda i, j: (j, 0, 0)),
            pl.BlockSpec((1, D_MODEL), lambda i, j: (0, 0)),
        ],
        out_specs=pl.BlockSpec((tm, D_MODEL), lambda i, j: (i, 0)),
        out_shape=jax.ShapeDtypeStruct((n, D_MODEL), F32),
        scratch_shapes=[pltpu.VMEM((tm, D_MODEL), BF16), pltpu.VMEM((tm, D_MODEL), F32)],
        compiler_params=_params(("parallel", "arbitrary")),
        name="ffn",
    )(h, g_pre, wgu, wd, g_post)


def _inproj_body(h_ref, g_ref, w_ref, ssd_ref, mla_ref, gdn_ref, small_ref):
    xn = _rms(h_ref[...], g_ref[...]).astype(BF16)
    y = _dot(xn, w_ref[...])
    ssd_ref[...] = y[:, 0:1280]
    mla_ref[...] = y[:, 1280:1920]
    gdn_ref[...] = y[:, 1920:2944]
    small_ref[...] = y[:, 2944:3072]


def _inproj(h, g, w, tm):
    n = h.shape[0]
    tm = min(tm, n)
    row = lambda i: (i, 0)
    fixed = lambda i: (0, 0)
    return pl.pallas_call(
        _inproj_body,
        grid=(n // tm,),
        in_specs=[
            pl.BlockSpec((tm, D_MODEL), row),
            pl.BlockSpec((1, D_MODEL), fixed),
            pl.BlockSpec((D_MODEL, IN_COLS_PADDED), fixed),
        ],
        out_specs=[
            pl.BlockSpec((tm, SSD_MAIN), row),
            pl.BlockSpec((tm, MLA_MAIN), row),
            pl.BlockSpec((tm, GDN_MAIN), row),
            pl.BlockSpec((tm, LANES), row),
        ],
        out_shape=[
            jax.ShapeDtypeStruct((n, SSD_MAIN), F32),
            jax.ShapeDtypeStruct((n, MLA_MAIN), F32),
            jax.ShapeDtypeStruct((n, GDN_MAIN), F32),
            jax.ShapeDtypeStruct((n, LANES), F32),
        ],
        compiler_params=_params(("parallel",)),
        name="inproj",
    )(h, g, w)


def _causal_conv(x, xpad_ref, cw_ref, tile):
    xpad_ref[8:8 + tile, :] = x
    acc = cw_ref[CONV_K - 1:CONV_K, :] * x
    for k in range(CONV_K - 1):
        off = 8 - (CONV_K - 1) + k
        acc = acc + cw_ref[k:k + 1, :] * xpad_ref[off:off + tile, :]
    xpad_ref[0:8, :] = x[tile - 8:tile, :]
    return acc


def _ssd_body(main_ref, small_ref, cw_ref, cb_ref, dtb_ref, alog_ref, dsk_ref, ng_ref,
              y_ref, xpad_ref, st_ref, *, L):
    t = pl.program_id(1)

    @pl.when(t == 0)
    def _():
        xpad_ref[0:8, :] = jnp.zeros((8, SSD_CONV_DIM), F32)
        st_ref[...] = jnp.zeros_like(st_ref)

    z = main_ref[:, 0:SSD_WIDTH]
    xbc = main_ref[:, SSD_WIDTH:SSD_MAIN]
    xa = _silu(_causal_conv(xbc, xpad_ref, cw_ref, L) + cb_ref[...])
    xs = xa[:, 0:512]
    bm = xa[:, 512:640]
    cm = xa[:, 640:768]

    dt = _softplus(small_ref[...] + dtb_ref[...])
    dta = dt * (-jnp.exp(alog_ref[...]))
    ri = _iota((L, L), 0)
    ci = _iota((L, L), 1)
    causal = ri >= ci
    acum = _dot_exact(causal.astype(F32), dta)
    expand = (_iota((LANES, SSD_WIDTH), 0) == _iota((LANES, SSD_WIDTH), 1) // SSD_HEAD_DIM).astype(F32)
    dt_x = _dot_exact(dt, expand)
    acum_x = _dot_exact(acum, expand)
    acum_t = acum.T
    bm_t = bm.T.astype(BF16)
    alast_x = acum_x[L - 1:L, :]
    xdt = xs * dt_x

    state_in = st_ref[...]
    y_off = _dot(cm.astype(BF16), state_in.astype(BF16)) * jnp.exp(acum_x)
    w_end = (xdt * jnp.exp(alast_x - acum_x)).astype(BF16)
    chunk_state = _dot(bm_t, w_end)
    own_group = (_iota((LANES, SSD_WIDTH), 0) // SSD_STATE) == (_iota((LANES, SSD_WIDTH), 1) // 256)
    st_ref[...] = jnp.where(own_group, state_in * jnp.exp(alast_x) + chunk_state, 0.0)

    lane = _iota((L, LANES), 1)
    group_of_lane = _iota((1, LANES), 1) // SSD_STATE
    parts = []
    for g in range(SSD_GROUPS):
        cb = _dot(jnp.where(group_of_lane == g, cm, 0.0).astype(BF16), bm_t)
        for pair in range(2):
            slot = g * 2 + pair
            xp = xdt[:, slot * LANES:(slot + 1) * LANES].astype(BF16)
            res = []
            for k in range(2):
                hh = slot * 2 + k
                seg = jnp.where(causal, acum[:, hh:hh + 1] - acum_t[hh:hh + 1, :], NEG_BIG)
                res.append(_dot((cb * jnp.exp(seg)).astype(BF16), xp))
            parts.append(jnp.where(lane < SSD_HEAD_DIM, res[0], res[1]))
    y = jnp.concatenate(parts, axis=1) + y_off + xs * dsk_ref[...]
    y = y * _silu(z)
    outs = []
    for g in range(SSD_GROUPS):
        yg = y[:, g * 256:(g + 1) * 256]
        outs.append(yg * lax.rsqrt(jnp.mean(yg * yg, axis=-1, keepdims=True) + NORM_EPS))
    y_ref[...] = (jnp.concatenate(outs, axis=1) * ng_ref[...]).astype(BF16)


def _ssd(main, small, cw, cb, dtb, alog, dsk, ng, bsz, seq, L):
    L = min(L, seq)
    nt = seq // L
    row = lambda b, t: (b * nt + t, 0)
    fixed = lambda b, t: (0, 0)
    return pl.pallas_call(
        functools.partial(_ssd_body, L=L),
        grid=(bsz, nt),
        in_specs=[
            pl.BlockSpec((L, SSD_MAIN), row),
            pl.BlockSpec((L, LANES), row),
            pl.BlockSpec((CONV_K, SSD_CONV_DIM), fixed),
            pl.BlockSpec((1, SSD_CONV_DIM), fixed),
            pl.BlockSpec((1, LANES), fixed),
            pl.BlockSpec((1, LANES), fixed),
            pl.BlockSpec((1, SSD_WIDTH), fixed),
            pl.BlockSpec((1, SSD_WIDTH), fixed),
        ],
        out_specs=pl.BlockSpec((L, SSD_WIDTH), row),
        out_shape=jax.ShapeDtypeStruct((bsz * seq, SSD_WIDTH), BF16),
        scratch_shapes=[pltpu.VMEM((8 + L, SSD_CONV_DIM), F32), pltpu.VMEM((LANES, SSD_WIDTH), F32)],
        compiler_params=_params(("parallel", "arbitrary")),
        name="ssd",
    )(main, small, cw, cb, dtb, alog, dsk, ng)


def _gdn_body(main_ref, small_ref, cw_ref, dtb_ref, alog_ref, ng_ref,
              y_ref, xpad_ref, s_ref, vnew_ref, *, T):
    t = pl.program_id(1)

    @pl.when(t == 0)
    def _():
        xpad_ref[0:8, :] = jnp.zeros((8, GDN_CONV_DIM), F32)
        s_ref[...] = jnp.zeros_like(s_ref)

    qkv_raw = main_ref[:, 0:GDN_CONV_DIM]
    z = main_ref[:, GDN_CONV_DIM:GDN_MAIN]
    xa = _silu(_causal_conv(qkv_raw, xpad_ref, cw_ref, T))
    q = xa[:, 0:256]
    k = xa[:, 256:512]
    v = xa[:, 512:768]

    same_head = (_iota((256, 256), 0) // GDN_DK) == (_iota((256, 256), 1) // GDN_DK)
    head_ones = same_head.astype(F32)
    qn = q * lax.rsqrt(_dot_exact(q * q, head_ones) + NORM_EPS) * (GDN_DK ** -0.5)
    kn = k * lax.rsqrt(_dot_exact(k * k, head_ones) + NORM_EPS)

    sm = small_ref[...]
    beta = jax.nn.sigmoid(sm)
    gl = -jnp.exp(alog_ref[...]) * _softplus(sm + dtb_ref[...])
    ri = _iota((T, T), 0)
    ci = _iota((T, T), 1)
    same_chunk = (ri // CHUNK) == (ci // CHUNK)
    incl = jnp.logical_and(same_chunk, ri >= ci)
    strict = jnp.logical_and(same_chunk, ri > ci)
    gcum = _dot_exact(incl.astype(F32), gl)
    er = _iota((LANES, GDN_WIDTH), 0)
    ec = _iota((LANES, GDN_WIDTH), 1) // GDN_DK
    beta_x = _dot_exact(beta, (er == ec + SMALL_B).astype(F32))
    gcum_x = _dot_exact(gcum, (er == ec + SMALL_A).astype(F32))
    last_sel = (ci == (ri // CHUNK) * CHUNK + (CHUNK - 1)).astype(F32)
    glast_x = _dot_exact(last_sel, gcum_x)
    gcum_t = gcum.T

    kb = kn * beta_x
    eg = jnp.exp(gcum_x)
    rhs = jnp.concatenate([v * beta_x, kb * eg], axis=1).astype(BF16)
    q_dec = (qn * eg).astype(BF16)
    k_dec_t = (kn * jnp.exp(glast_x - gcum_x)).T
    kn_t = kn.T.astype(BF16)
    head_of_lane = _iota((1, GDN_WIDTH), 1) // GDN_DK
    eye = (ri == ci).astype(F32)
    lower_left = []
    b = 1
    while b < CHUNK:
        in_block = (ri // (2 * b)) == (ci // (2 * b))
        lower_left.append(jnp.logical_and(in_block, (ri // b) % 2 > (ci // b) % 2))
        b *= 2

    heads = range(GDN_HEADS)
    hms = [head_of_lane == h for h in heads]
    gammas = [jnp.exp(jnp.where(incl, gcum[:, SMALL_A + h:SMALL_A + h + 1] - gcum_t[SMALL_A + h:SMALL_A + h + 1, :],
                                NEG_BIG)) for h in heads]
    kks = [_dot(jnp.where(hms[h], kb, 0.0).astype(BF16), kn_t) for h in heads]
    mlows = [jnp.where(strict, kks[h] * gammas[h], 0.0) for h in heads]
    ps = [eye - jnp.where(lower_left[0], mlows[h], 0.0) for h in heads]
    for ll in lower_left[1:]:
        pbs = [p.astype(BF16) for p in ps]
        ts = [_dot(pbs[h], jnp.where(ll, mlows[h], 0.0).astype(BF16)).astype(BF16) for h in heads]
        ps = [ps[h] - _dot(ts[h], pbs[h]) for h in heads]
    sols = [_dot(ps[h].astype(BF16), rhs) for h in heads]
    u = sols[0][:, 0:256]
    w = sols[0][:, 256:512]
    for h in range(1, GDN_HEADS):
        u = jnp.where(hms[h], sols[h][:, 0:256], u)
        w = jnp.where(hms[h], sols[h][:, 256:512], w)
    qks = [_dot(jnp.where(hms[h], qn, 0.0).astype(BF16), kn_t) for h in heads]
    qk_heads = [jnp.where(incl, qks[h] * gammas[h], 0.0).astype(BF16) for h in heads]

    vnew_ref[...] = jnp.zeros((T, GDN_WIDTH), F32)
    chunk_of_col = _iota((1, T), 1) // CHUNK
    o_inter = []
    for c in range(T // CHUNK):
        r0 = c * CHUNK
        s_in = s_ref[...]
        s_b = s_in.astype(BF16)
        vnew_ref[r0:r0 + CHUNK, :] = u[r0:r0 + CHUNK, :] - _dot(w[r0:r0 + CHUNK, :].astype(BF16), s_b)
        o_inter.append(_dot(q_dec[r0:r0 + CHUNK, :], s_b))
        kd = jnp.where(chunk_of_col == c, k_dec_t, 0.0).astype(BF16)
        upd = _dot(kd, vnew_ref[...].astype(BF16))
        dec = jnp.exp(glast_x[r0:r0 + 1, :])
        s_ref[...] = jnp.where(same_head, s_in * dec + upd, 0.0)

    vnew = vnew_ref[...].astype(BF16)
    o = jnp.concatenate(o_inter, axis=0)
    for h in range(GDN_HEADS):
        o = o + jnp.where(head_of_lane == h, _dot(qk_heads[h], vnew), 0.0)
    ms = _dot_exact(o * o, head_ones) * (1.0 / GDN_DV)
    y = o * lax.rsqrt(ms + NORM_EPS) * ng_ref[...] * _silu(z)
    y_ref[...] = y.astype(BF16)


def _gdn(main, small, cw, dtb, alog, ng, bsz, seq, T):
    T = min(T, seq)
    nt = seq // T
    row = lambda b, t: (b * nt + t, 0)
    fixed = lambda b, t: (0, 0)
    return pl.pallas_call(
        functools.partial(_gdn_body, T=T),
        grid=(bsz, nt),
        in_specs=[
            pl.BlockSpec((T, GDN_MAIN), row),
            pl.BlockSpec((T, LANES), row),
            pl.BlockSpec((CONV_K, GDN_CONV_DIM), fixed),
            pl.BlockSpec((1, LANES), fixed),
            pl.BlockSpec((1, LANES), fixed),
            pl.BlockSpec((1, GDN_WIDTH), fixed),
        ],
        out_specs=pl.BlockSpec((T, GDN_WIDTH), row),
        out_shape=jax.ShapeDtypeStruct((bsz * seq, GDN_WIDTH), BF16),
        scratch_shapes=[
            pltpu.VMEM((8 + T, GDN_CONV_DIM), F32),
            pltpu.VMEM((GDN_WIDTH, GDN_WIDTH), F32),
            pltpu.VMEM((T, GDN_WIDTH), F32),
        ],
        compiler_params=_params(("parallel", "arbitrary")),
        name="gdn",
    )(main, small, cw, dtb, alog, ng)


def _mla_prep_body(c_ref, cos_ref, sin_ref, gq_ref, gkv_ref, wq_ref, wkv_ref, vone_ref,
                   qt_ref, k_ref, vt_ref):
    c = c_ref[...]
    cq = _rms(c[:, 0:256], gq_ref[...]).astype(BF16)
    ckv = _rms(c[:, 256:384], gkv_ref[...]).astype(BF16)
    cos_t = cos_ref[...]
    sin_t = sin_ref[...]
    k_rot = c[:, 384:512] * cos_t + c[:, 512:640] * sin_t
    qq = _dot(cq, wq_ref[...])
    kvv = _dot(ckv, wkv_ref[...])
    width = MLA_HEADS * HEAD_PAD
    qs, ks = [], []
    for h in range(MLA_HEADS):
        lo, hi = h * HEAD_PAD, (h + 1) * HEAD_PAD
        qs.append((qq[:, lo:hi] * cos_t + qq[:, width + lo:width + hi] * sin_t) * MLA_Q_SCALE)
        ks.append(kvv[:, lo:hi] + k_rot)
    qt_ref[0, 0] = jnp.concatenate(qs, axis=1).T.astype(BF16)
    k_ref[...] = jnp.concatenate(ks, axis=1).astype(BF16)
    vt_ref[0, 0] = (kvv[:, width:2 * width] + vone_ref[...]).T.astype(BF16)


def _mla_prep(cols, cos_t, sin_t, gq, gkv, wq, wkv, vone, bsz, seq, tm):
    tm = min(tm, seq)
    nt = seq // tm
    width = MLA_HEADS * HEAD_PAD
    row = lambda i: (i, 0)
    fixed = lambda i: (0, 0)
    return pl.pallas_call(
        _mla_prep_body,
        grid=(bsz * nt,),
        in_specs=[
            pl.BlockSpec((tm, MLA_MAIN), row),
            pl.BlockSpec((tm, LANES), row),
            pl.BlockSpec((tm, LANES), row),
            pl.BlockSpec((1, MLA_Q_LORA), fixed),
            pl.BlockSpec((1, MLA_KV_LORA), fixed),
            pl.BlockSpec((MLA_Q_LORA, 2 * width), fixed),
            pl.BlockSpec((MLA_KV_LORA, 2 * width), fixed),
            pl.BlockSpec((1, width), fixed),
        ],
        out_specs=[
            pl.BlockSpec((1, 1, width, tm), lambda i: (i // nt, i % nt, 0, 0)),
            pl.BlockSpec((tm, width), row),
            pl.BlockSpec((1, 1, width, tm), lambda i: (i // nt, i % nt, 0, 0)),
        ],
        out_shape=[
            jax.ShapeDtypeStruct((bsz, nt, width, tm), BF16),
            jax.ShapeDtypeStruct((bsz * seq, width), BF16),
            jax.ShapeDtypeStruct((bsz, nt, width, tm), BF16),
        ],
        compiler_params=_params(("parallel",)),
        name="mla_prep",
    )(cols, cos_t, sin_t, gq, gkv, wq, wkv, vone)


def _mla_attn_body(qt_ref, k_ref, vt_ref, o_ref, *, tq):
    i = pl.program_id(1)
    visible = (_iota((tq, tq), 0) // CHUNK) <= (_iota((tq, tq), 1) // CHUNK)
    qts = [qt_ref[0, 0, h * HEAD_PAD:(h + 1) * HEAD_PAD, :] for h in range(MLA_HEADS)]

    def step(j, carry, diagonal):
        heads = range(MLA_HEADS)
        kblk = k_ref[0, pl.ds(pl.multiple_of(j * tq, tq), tq), :]
        ss = [_dot(kblk[:, h * HEAD_PAD:(h + 1) * HEAD_PAD], qts[h]) for h in heads]
        if diagonal:
            ss = [jnp.where(visible, s, NEG_BIG) for s in ss]
        m_new = [jnp.maximum(carry[h][0], jnp.max(ss[h], axis=0, keepdims=True)) for h in heads]
        ps = [jnp.exp2(ss[h] - m_new[h]).astype(BF16) for h in heads]
        pv = [_dot(vt_ref[0, j, h * HEAD_PAD:(h + 1) * HEAD_PAD, :], ps[h]) for h in heads]
        return tuple((m_new[h], jnp.exp2(carry[h][0] - m_new[h]) * carry[h][1] + pv[h]) for h in heads)

    init = tuple((jnp.full((1, tq), NEG_BIG, F32), jnp.zeros((HEAD_PAD, tq), F32)) for _ in range(MLA_HEADS))
    carry = lax.fori_loop(0, i, functools.partial(step, diagonal=False), init)
    carry = step(i, carry, True)
    outs = []
    for h in range(MLA_HEADS):
        acc = carry[h][1]
        outs.append((acc / acc[MLA_V:MLA_V + 1, :]).T)
    o_ref[...] = jnp.concatenate(outs, axis=1).astype(BF16)


def _mla_attn(qt, k, vt, bsz, seq, tq):
    tq = min(tq, seq)
    nq = seq // tq
    width = MLA_HEADS * HEAD_PAD
    return pl.pallas_call(
        functools.partial(_mla_attn_body, tq=tq),
        grid=(bsz, nq),
        in_specs=[
            pl.BlockSpec((1, 1, width, tq), lambda b, i: (b, i, 0, 0)),
            pl.BlockSpec((1, seq, width), lambda b, i: (b, 0, 0)),
            pl.BlockSpec((1, nq, width, tq), lambda b, i: (b, 0, 0, 0)),
        ],
        out_specs=pl.BlockSpec((tq, width), lambda b, i: (b * nq + i, 0)),
        out_shape=jax.ShapeDtypeStruct((bsz * seq, width), BF16),
        compiler_params=_params(("parallel", "arbitrary")),
        name="mla_attn",
    )(qt, k.reshape(bsz, seq, width), vt)


def _memkv_body(mem_ref, g_ref, w_ref, kt_ref, v_ref):
    mn = _rms(mem_ref[...], g_ref[...]).astype(BF16)
    kv = _dot(mn, w_ref[...])
    kt_ref[0] = kv[:, 0:D_MODEL].T.astype(BF16)
    v_ref[0] = kv[:, D_MODEL:2 * D_MODEL].astype(BF16)


def _memkv(mem2d, g, w, bsz):
    return pl.pallas_call(
        _memkv_body,
        grid=(bsz,),
        in_specs=[
            pl.BlockSpec((N_MEM, D_MODEL), lambda b: (b, 0)),
            pl.BlockSpec((1, D_MODEL), lambda b: (0, 0)),
            pl.BlockSpec((D_MODEL, 2 * D_MODEL), lambda b: (0, 0)),
        ],
        out_specs=[
            pl.BlockSpec((1, D_MODEL, N_MEM), lambda b: (b, 0, 0)),
            pl.BlockSpec((1, N_MEM, D_MODEL), lambda b: (b, 0, 0)),
        ],
        out_shape=[
            jax.ShapeDtypeStruct((bsz, D_MODEL, N_MEM), BF16),
            jax.ShapeDtypeStruct((bsz, N_MEM, D_MODEL), BF16),
        ],
        compiler_params=_params(("parallel",)),
        name="memkv",
    )(mem2d, g, w)


def _post_body(h_ref, ys_ref, ym_ref, yg_ref, wos_ref, wom_ref, wog_ref, gmix_ref, gxa_ref,
               wq_ref, kt_ref, v_ref, wo_ref, gpost_ref, o_ref):
    mix = _dot(ys_ref[...], wos_ref[...]) + _dot(ym_ref[...], wom_ref[...]) + _dot(yg_ref[...], wog_ref[...])
    h1 = h_ref[...] + _rms(mix, gmix_ref[...])
    u = _rms(h1, gxa_ref[...]).astype(BF16)
    q = (_dot(u, wq_ref[...]) * (XA_HEAD_DIM ** -0.5)).astype(BF16)
    outs = []
    for hd in range(XA_HEADS):
        lo, hi = hd * XA_HEAD_DIM, (hd + 1) * XA_HEAD_DIM
        s = _dot(q[:, lo:hi], kt_ref[0, lo:hi, :])
        p = jnp.exp(s - jnp.max(s, axis=-1, keepdims=True))
        o = _dot(p.astype(BF16), v_ref[0, :, lo:hi])
        outs.append((o / jnp.sum(p, axis=-1, keepdims=True)).astype(BF16))
    xa = _dot(jnp.concatenate(outs, axis=1), wo_ref[...])
    o_ref[...] = h1 + _rms(xa, gpost_ref[...])


def _post(h, ys, ym, yg, wos, wom, wog, gmix, gxa, wq, kt, v, wo, gpost, bsz, seq, tm):
    tm = min(tm, seq)
    nt = seq // tm
    row = lambda b, t: (b * nt + t, 0)
    fixed = lambda b, t: (0, 0)
    full = lambda a: pl.BlockSpec(a.shape, fixed)
    return pl.pallas_call(
        _post_body,
        grid=(bsz, nt),
        in_specs=[
            pl.BlockSpec((tm, D_MODEL), row),
            pl.BlockSpec((tm, ys.shape[1]), row),
            pl.BlockSpec((tm, ym.shape[1]), row),
            pl.BlockSpec((tm, yg.shape[1]), row),
            full(wos), full(wom), full(wog), full(gmix), full(gxa), full(wq),
            pl.BlockSpec((1, D_MODEL, N_MEM), lambda b, t: (b, 0, 0)),
            pl.BlockSpec((1, N_MEM, D_MODEL), lambda b, t: (b, 0, 0)),
            full(wo), full(gpost),
        ],
        out_specs=pl.BlockSpec((tm, D_MODEL), row),
        out_shape=jax.ShapeDtypeStruct((bsz * seq, D_MODEL), F32),
        compiler_params=_params(("parallel", "arbitrary")),
        name="mix_out_xattn",
    )(h, ys, ym, yg, wos, wom, wog, gmix, gxa, wq, kt, v, wo, gpost)


def _pad_lanes(vals, offset, width=LANES):
    n = vals.shape[-1]
    pad = [(0, 0)] * (vals.ndim - 1) + [(offset, width - offset - n)]
    return jnp.pad(vals, pad)[..., None, :]


def _prep_params(norm_g, ffn_w_up, ffn_w_down, w_in, ssd_conv_w, ssd_conv_b, ssd_dt_bias, ssd_a_log,
                 ssd_d, ssd_norm_g, mla_q_norm_g, mla_w_uq, mla_kv_norm_g, mla_w_ukv, gdn_conv_w,
                 gdn_dt_bias, gdn_a_log, gdn_norm_g, w_out, xa_w_q, xa_w_kv, xa_w_o, tf):
    nj = D_FF // tf
    depth = norm_g.shape[0]
    gate = ffn_w_up[..., :D_FF].reshape(depth, 2, D_MODEL, nj, 1, tf)
    up = ffn_w_up[..., D_FF:].reshape(depth, 2, D_MODEL, nj, 1, tf)
    wgu = jnp.concatenate([gate, up], axis=4).astype(BF16)
    wgu = jnp.transpose(wgu, (0, 1, 3, 2, 4, 5)).reshape(depth, 2, nj, D_MODEL, 2 * tf)
    vone = np.zeros((1, MLA_HEADS * HEAD_PAD), np.float32)
    vone[0, np.arange(MLA_HEADS) * HEAD_PAD + MLA_V] = 1.0
    return dict(
        norm_g=norm_g[:, :, None, :],
        wgu=wgu,
        wd=ffn_w_down.astype(BF16).reshape(depth, 2, nj, tf, D_MODEL),
        w_in=_gather_cols(w_in, _IN_PERM).astype(BF16),
        ssd_cw=ssd_conv_w,
        ssd_cb=ssd_conv_b[:, None, :],
        ssd_dtb=_pad_lanes(ssd_dt_bias, SMALL_DT),
        ssd_alog=_pad_lanes(ssd_a_log, SMALL_DT),
        ssd_dsk=jnp.repeat(ssd_d, SSD_HEAD_DIM, axis=-1)[:, None, :],
        ssd_ng=ssd_norm_g[:, None, :],
        mla_gq=mla_q_norm_g[:, None, :],
        mla_gkv=mla_kv_norm_g[:, None, :],
        mla_wq=_gather_cols(mla_w_uq, _UQ_PERM).astype(BF16),
        mla_wkv=_gather_cols(mla_w_ukv, _UKV_PERM).astype(BF16),
        mla_vone=jnp.broadcast_to(jnp.asarray(vone), (depth,) + vone.shape),
        gdn_cw=gdn_conv_w,
        gdn_dtb=_pad_lanes(gdn_dt_bias, SMALL_A),
        gdn_alog=_pad_lanes(gdn_a_log, SMALL_A),
        gdn_ng=jnp.tile(gdn_norm_g, (1, GDN_HEADS))[:, None, :],
        wo_ssd=w_out[:, 0:SSD_WIDTH, :].astype(BF16),
        wo_mla=jnp.swapaxes(_gather_cols(jnp.swapaxes(w_out, 1, 2), _WOUT_MLA_ROWS), 1, 2).astype(BF16),
        wo_gdn=w_out[:, SSD_WIDTH + MLA_HEADS * MLA_V:, :].astype(BF16),
        xa_wq=xa_w_q.astype(BF16),
        xa_wkv=xa_w_kv.astype(BF16),
        xa_wo=xa_w_o.astype(BF16),
    )


def _rope_tables(positions):
    inv = 1.0 / (ROPE_THETA ** (jnp.arange(0, MLA_ROPE, 2, dtype=F32) / MLA_ROPE))
    ang = positions.astype(F32).reshape(-1, 1) * inv
    cos, sin = jnp.cos(ang), jnp.sin(ang)
    n = ang.shape[0]
    ones = jnp.ones((n, MLA_NOPE), F32)
    zeros_nope = jnp.zeros((n, MLA_NOPE), F32)
    zeros_pad = jnp.zeros((n, HEAD_PAD - MLA_NOPE - MLA_ROPE), F32)
    cos_t = jnp.concatenate([ones, cos, cos, zeros_pad], axis=1)
    sin_t = jnp.concatenate([zeros_nope, -sin, sin, zeros_pad], axis=1)
    return cos_t, sin_t


FFN_TM = 1024
FFN_TF = 256
TOKEN_TM = 512
SEQ_TILE = 256


def kernel(x, mem, positions, norm_g, ffn_w_up, ffn_w_down, w_in, ssd_conv_w, ssd_conv_b, ssd_dt_bias,
           ssd_a_log, ssd_d, ssd_norm_g, mla_q_norm_g, mla_w_uq, mla_kv_norm_g, mla_w_ukv, gdn_conv_w,
           gdn_dt_bias, gdn_a_log, gdn_norm_g, w_out, xa_w_q, xa_w_kv, xa_w_o):
    bsz, seq, _ = x.shape
    n = bsz * seq
    cos_t, sin_t = _rope_tables(positions)
    params = _prep_params(norm_g, ffn_w_up, ffn_w_down, w_in, ssd_conv_w, ssd_conv_b, ssd_dt_bias,
                          ssd_a_log, ssd_d, ssd_norm_g, mla_q_norm_g, mla_w_uq, mla_kv_norm_g, mla_w_ukv,
                          gdn_conv_w, gdn_dt_bias, gdn_a_log, gdn_norm_g, w_out, xa_w_q, xa_w_kv, xa_w_o,
                          FFN_TF)
    mem2d = mem.reshape(bsz * N_MEM, D_MODEL)

    def layer(h, p):
        g = p["norm_g"]
        h = _ffn(h, g[FFN1_PRE], p["wgu"][0], p["wd"][0], g[FFN1_POST], FFN_TM)
        c_ssd, c_mla, c_gdn, c_small = _inproj(h, g[MIX_PRE], p["w_in"], TOKEN_TM)
        y_ssd = _ssd(c_ssd, c_small, p["ssd_cw"], p["ssd_cb"], p["ssd_dtb"], p["ssd_alog"], p["ssd_dsk"],
                     p["ssd_ng"], bsz, seq, SEQ_TILE)
        qt, kk, vt = _mla_prep(c_mla, cos_t, sin_t, p["mla_gq"], p["mla_gkv"], p["mla_wq"], p["mla_wkv"],
                               p["mla_vone"], bsz, seq, SEQ_TILE)
        y_mla = _mla_attn(qt, kk, vt, bsz, seq, SEQ_TILE)
        y_gdn = _gdn(c_gdn, c_small, p["gdn_cw"], p["gdn_dtb"], p["gdn_alog"], p["gdn_ng"], bsz, seq, SEQ_TILE)
        mkt, mv = _memkv(mem2d, g[MEM_NORM], p["xa_wkv"], bsz)
        h = _post(h, y_ssd, y_mla, y_gdn, p["wo_ssd"], p["wo_mla"], p["wo_gdn"], g[MIX_POST], g[XA_PRE],
                  p["xa_wq"], mkt, mv, p["xa_wo"], g[XA_POST], bsz, seq, TOKEN_TM)
        h = _ffn(h, g[FFN2_PRE], p["wgu"][1], p["wd"][1], g[FFN2_POST], FFN_TM)
        return h, None

    h, _ = lax.scan(layer, x.reshape(n, D_MODEL), params)
    return h.reshape(bsz, seq, D_MODEL)
```

```python
import functools

import numpy as np
import jax
import jax.numpy as jnp
from jax import lax
from jax.experimental import pallas as pl
from jax.experimental.pallas import tpu as pltpu

F32 = jnp.float32
BF16 = jnp.bfloat16

D_MODEL = 1024
DEPTH = 4
CHUNK = 64
NORM_EPS = 1e-6
CONV_K = 4
D_FF = 2816
N_MEM = 256

SSD_HEADS = 8
SSD_HEAD_DIM = 64
SSD_WIDTH = 512
SSD_GROUPS = 2
SSD_STATE = 64
SSD_CONV_DIM = 768
SSD_IN = 1288

MLA_HEADS = 4
MLA_Q_LORA = 256
MLA_KV_LORA = 128
MLA_NOPE = 64
MLA_ROPE = 32
MLA_V = 64
MLA_IN = 416
ROPE_THETA = 10000.0
MLA_SCALE = float((MLA_NOPE + MLA_ROPE) ** -0.5)
MLA_Q_SCALE = MLA_SCALE * float(np.log2(np.e))

GDN_HEADS = 4
GDN_DK = 64
GDN_DV = 64
GDN_CONV_DIM = 768
GDN_WIDTH = 256
GDN_IN = 1032

XA_HEADS = 4
XA_HEAD_DIM = 256

(FFN1_PRE, FFN1_POST, MIX_PRE, MIX_POST, MEM_NORM, XA_PRE, XA_POST, FFN2_PRE, FFN2_POST) = range(9)

LANES = 128
HEAD_PAD = 128
NEG_BIG = -1e30
VMEM_LIMIT = 56 * 1024 * 1024

IN_COLS_PADDED = 3072
SSD_MAIN = 1280
MLA_MAIN = 640
GDN_MAIN = 1024
SMALL_DT, SMALL_B, SMALL_A = 0, 8, 12


def _in_perm():
    perm = -np.ones((IN_COLS_PADDED,), np.int64)
    perm[0:1280] = np.arange(0, 1280)
    mla0 = SSD_IN
    perm[1280:1664] = np.arange(mla0, mla0 + 384)
    kr0 = mla0 + 384
    half = MLA_ROPE // 2
    perm[1664 + 64:1664 + 96] = np.arange(kr0, kr0 + 32)
    perm[1792 + 64:1792 + 64 + half] = np.arange(kr0 + half, kr0 + 32)
    perm[1792 + 64 + half:1792 + 96] = np.arange(kr0, kr0 + half)
    gdn0 = SSD_IN + MLA_IN
    perm[1920:2944] = np.arange(gdn0, gdn0 + 1024)
    perm[2944 + SMALL_DT:2944 + SMALL_DT + 8] = np.arange(1280, 1288)
    perm[2944 + SMALL_B:2944 + SMALL_B + 4] = np.arange(gdn0 + 1024, gdn0 + 1028)
    perm[2944 + SMALL_A:2944 + SMALL_A + 4] = np.arange(gdn0 + 1028, gdn0 + 1032)
    return perm


_IN_PERM = _in_perm()


def _gather_cols(w, perm):
    valid = jnp.asarray(perm >= 0)
    idx = jnp.asarray(np.maximum(perm, 0))
    return jnp.where(valid, jnp.take(w, idx, axis=-1), 0.0)


def _uq_perms():
    a = -np.ones((MLA_HEADS * HEAD_PAD,), np.int64)
    b = -np.ones((MLA_HEADS * HEAD_PAD,), np.int64)
    hd = MLA_NOPE + MLA_ROPE
    half = MLA_ROPE // 2
    for h in range(MLA_HEADS):
        a[h * HEAD_PAD:h * HEAD_PAD + hd] = np.arange(h * hd, (h + 1) * hd)
        r0 = h * hd + MLA_NOPE
        b[h * HEAD_PAD + 64:h * HEAD_PAD + 64 + half] = np.arange(r0 + half, r0 + 32)
        b[h * HEAD_PAD + 64 + half:h * HEAD_PAD + 96] = np.arange(r0, r0 + half)
    return np.concatenate([a, b])


def _ukv_perms():
    k = -np.ones((MLA_HEADS * HEAD_PAD,), np.int64)
    v = -np.ones((MLA_HEADS * HEAD_PAD,), np.int64)
    hd = MLA_NOPE + MLA_V
    for h in range(MLA_HEADS):
        k[h * HEAD_PAD:h * HEAD_PAD + MLA_NOPE] = np.arange(h * hd, h * hd + MLA_NOPE)
        v[h * HEAD_PAD:h * HEAD_PAD + MLA_V] = np.arange(h * hd + MLA_NOPE, (h + 1) * hd)
    return np.concatenate([k, v])


_UQ_PERM = _uq_perms()
_UKV_PERM = _ukv_perms()


def _wout_mla_rows():
    rows = -np.ones((MLA_HEADS * HEAD_PAD,), np.int64)
    for h in range(MLA_HEADS):
        rows[h * HEAD_PAD:h * HEAD_PAD + MLA_V] = SSD_WIDTH + np.arange(h * MLA_V, (h + 1) * MLA_V)
    return rows


_WOUT_MLA_ROWS = _wout_mla_rows()


def _rms(x, g):
    return x * lax.rsqrt(jnp.mean(x * x, axis=-1, keepdims=True) + NORM_EPS) * g


def _silu(x):
    return x * jax.nn.sigmoid(x)


def _softplus(x):
    return jnp.maximum(x, 0.0) + jnp.log1p(jnp.exp(-jnp.abs(x)))


def _dot(a, b):
    return jnp.dot(a, b, preferred_element_type=F32)


def _split3(x):
    hi = x.astype(BF16)
    r1 = x - hi.astype(F32)
    mid = r1.astype(BF16)
    lo = (r1 - mid.astype(F32)).astype(BF16)
    return hi, mid, lo


def _dot_sel_right(x, sel):
    sel = sel.astype(BF16)
    hi, mid, lo = _split3(x)
    return _dot(hi, sel) + (_dot(mid, sel) + _dot(lo, sel))


def _dot_sel_left(sel, x):
    sel = sel.astype(BF16)
    hi, mid, lo = _split3(x)
    return _dot(sel, hi) + (_dot(sel, mid) + _dot(sel, lo))


def _iota(shape, dim):
    return lax.broadcasted_iota(jnp.int32, shape, dim)


def _params(sem):
    return pltpu.CompilerParams(dimension_semantics=sem, vmem_limit_bytes=VMEM_LIMIT)


def _ffn_body(h_ref, gpre_ref, wgu_ref, wd_ref, gpost_ref, o_ref, xn_ref, acc_ref, *, tf, nj):
    j = pl.program_id(1)

    @pl.when(j == 0)
    def _():
        xn_ref[...] = _rms(h_ref[...], gpre_ref[...]).astype(BF16)

    gu = _dot(xn_ref[...], wgu_ref[0])
    act = (_silu(gu[:, :tf]) * gu[:, tf:]).astype(BF16)
    part = _dot(act, wd_ref[0])

    @pl.when(j == 0)
    def _():
        acc_ref[...] = part

    @pl.when(j > 0)
    def _():
        acc_ref[...] += part

    @pl.when(j == nj - 1)
    def _():
        o_ref[...] = h_ref[...] + 0.5 * _rms(acc_ref[...], gpost_ref[...])


def _ffn(h, g_pre, wgu, wd, g_post, tm):
    n = h.shape[0]
    nj, _, tf2 = wgu.shape
    tf = tf2 // 2
    tm = min(tm, n)
    return pl.pallas_call(
        functools.partial(_ffn_body, tf=tf, nj=nj),
        grid=(n // tm, nj),
        in_specs=[
            pl.BlockSpec((tm, D_MODEL), lambda i, j: (i, 0)),
            pl.BlockSpec((1, D_MODEL), lambda i, j: (0, 0)),
            pl.BlockSpec((1, D_MODEL, tf2), lambda i, j: (j, 0, 0)),
            pl.BlockSpec((1, tf, D_MODEL), lambda i, j: (j, 0, 0)),
            pl.BlockSpec((1, D_MODEL), lambda i, j: (0, 0)),
        ],
        out_specs=pl.BlockSpec((tm, D_MODEL), lambda i, j: (i, 0)),
        out_shape=jax.ShapeDtypeStruct((n, D_MODEL), F32),
        scratch_shapes=[pltpu.VMEM((tm, D_MODEL), BF16), pltpu.VMEM((tm, D_MODEL), F32)],
        compiler_params=_params(("parallel", "arbitrary")),
        name="ffn",
    )(h, g_pre, wgu, wd, g_post)


def _inproj_body(h_ref, g_ref, w_ref, ssd_ref, mla_ref, gdn_ref, small_ref):
    xn = _rms(h_ref[...], g_ref[...]).astype(BF16)
    y = _dot(xn, w_ref[...])
    ssd_ref[...] = y[:, 0:1280]
    mla_ref[...] = y[:, 1280:1920]
    gdn_ref[...] = y[:, 1920:2944]
    small_ref[...] = y[:, 2944:3072]


def _inproj(h, g, w, tm):
    n = h.shape[0]
    tm = min(tm, n)
    row = lambda i: (i, 0)
    fixed = lambda i: (0, 0)
    return pl.pallas_call(
        _inproj_body,
        grid=(n // tm,),
        in_specs=[
            pl.BlockSpec((tm, D_MODEL), row),
            pl.BlockSpec((1, D_MODEL), fixed),
            pl.BlockSpec((D_MODEL, IN_COLS_PADDED), fixed),
        ],
        out_specs=[
            pl.BlockSpec((tm, SSD_MAIN), row),
            pl.BlockSpec((tm, MLA_MAIN), row),
            pl.BlockSpec((tm, GDN_MAIN), row),
            pl.BlockSpec((tm, LANES), row),
        ],
        out_shape=[
            jax.ShapeDtypeStruct((n, SSD_MAIN), F32),
            jax.ShapeDtypeStruct((n, MLA_MAIN), F32),
            jax.ShapeDtypeStruct((n, GDN_MAIN), F32),
            jax.ShapeDtypeStruct((n, LANES), F32),
        ],
        compiler_params=_params(("parallel",)),
        name="inproj",
    )(h, g, w)


def _causal_conv(x, xpad_ref, cw_ref, tile):
    xpad_ref[8:8 + tile, :] = x
    acc = cw_ref[CONV_K - 1:CONV_K, :] * x
    for k in range(CONV_K - 1):
        off = 8 - (CONV_K - 1) + k
        acc = acc + cw_ref[k:k + 1, :] * xpad_ref[off:off + tile, :]
    xpad_ref[0:8, :] = x[tile - 8:tile, :]
    return acc


def _ssd_body(main_ref, small_ref, cw_ref, cb_ref, dtb_ref, alog_ref, dsk_ref, ng_ref,
              y_ref, xpad_ref, st_ref, *, L):
    t = pl.program_id(1)

    @pl.when(t == 0)
    def _():
        xpad_ref[0:8, :] = jnp.zeros((8, SSD_CONV_DIM), F32)
        st_ref[...] = jnp.zeros_like(st_ref)

    z = main_ref[:, 0:SSD_WIDTH]
    xbc = main_ref[:, SSD_WIDTH:SSD_MAIN]
    xa = _silu(_causal_conv(xbc, xpad_ref, cw_ref, L) + cb_ref[...])
    xs = xa[:, 0:512]
    bm = xa[:, 512:640]
    cm = xa[:, 640:768]

    dt = _softplus(small_ref[...] + dtb_ref[...])
    dta = dt * (-jnp.exp(alog_ref[...]))
    ri = _iota((L, L), 0)
    ci = _iota((L, L), 1)
    causal = ri >= ci
    acum = _dot_sel_left(causal, dta)
    expand = (_iota((LANES, SSD_WIDTH), 0) == _iota((LANES, SSD_WIDTH), 1) // SSD_HEAD_DIM).astype(F32)
    dt_x = _dot_sel_right(dt, expand)
    acum_x = _dot_sel_right(acum, expand)
    acum_t = acum.T
    bm_t = bm.T.astype(BF16)
    alast_x = acum_x[L - 1:L, :]
    xdt = xs * dt_x

    state_in = st_ref[...]
    y_off = _dot(cm.astype(BF16), state_in.astype(BF16)) * jnp.exp(acum_x)
    w_end = (xdt * jnp.exp(alast_x - acum_x)).astype(BF16)
    chunk_state = _dot(bm_t, w_end)
    own_group = (_iota((LANES, SSD_WIDTH), 0) // SSD_STATE) == (_iota((LANES, SSD_WIDTH), 1) // 256)
    st_ref[...] = jnp.where(own_group, state_in * jnp.exp(alast_x) + chunk_state, 0.0)

    lane = _iota((L, LANES), 1)
    group_of_lane = _iota((1, LANES), 1) // SSD_STATE
    heads = range(SSD_HEADS)
    hpg = SSD_HEADS // SSD_GROUPS
    cbs = [_dot(jnp.where(group_of_lane == g, cm, 0.0).astype(BF16), bm_t) for g in range(SSD_GROUPS)]
    segs = [jnp.where(causal, acum[:, hh:hh + 1] - acum_t[hh:hh + 1, :], NEG_BIG) for hh in heads]
    gs = [(cbs[hh // hpg] * jnp.exp(segs[hh])).astype(BF16) for hh in heads]
    xps = [xdt[:, slot * LANES:(slot + 1) * LANES].astype(BF16) for slot in range(SSD_HEADS // 2)]
    res = [_dot(gs[hh], xps[hh // 2]) for hh in heads]
    parts = [jnp.where(lane < SSD_HEAD_DIM, res[2 * slot], res[2 * slot + 1]) for slot in range(SSD_HEADS // 2)]
    y = jnp.concatenate(parts, axis=1) + y_off + xs * dsk_ref[...]
    y = y * _silu(z)
    outs = []
    for g in range(SSD_GROUPS):
        yg = y[:, g * 256:(g + 1) * 256]
        outs.append(yg * lax.rsqrt(jnp.mean(yg * yg, axis=-1, keepdims=True) + NORM_EPS))
    y_ref[...] = (jnp.concatenate(outs, axis=1) * ng_ref[...]).astype(BF16)


def _ssd(main, small, cw, cb, dtb, alog, dsk, ng, bsz, seq, L):
    L = min(L, seq)
    nt = seq // L
    row = lambda b, t: (b * nt + t, 0)
    fixed = lambda b, t: (0, 0)
    return pl.pallas_call(
        functools.partial(_ssd_body, L=L),
        grid=(bsz, nt),
        in_specs=[
            pl.BlockSpec((L, SSD_MAIN), row),
            pl.BlockSpec((L, LANES), row),
            pl.BlockSpec((CONV_K, SSD_CONV_DIM), fixed),
            pl.BlockSpec((1, SSD_CONV_DIM), fixed),
            pl.BlockSpec((1, LANES), fixed),
            pl.BlockSpec((1, LANES), fixed),
            pl.BlockSpec((1, SSD_WIDTH), fixed),
            pl.BlockSpec((1, SSD_WIDTH), fixed),
        ],
        out_specs=pl.BlockSpec((L, SSD_WIDTH), row),
        out_shape=jax.ShapeDtypeStruct((bsz * seq, SSD_WIDTH), BF16),
        scratch_shapes=[pltpu.VMEM((8 + L, SSD_CONV_DIM), F32), pltpu.VMEM((LANES, SSD_WIDTH), F32)],
        compiler_params=_params(("parallel", "arbitrary")),
        name="ssd",
    )(main, small, cw, cb, dtb, alog, dsk, ng)


def _gdn_body(main_ref, small_ref, cw_ref, dtb_ref, alog_ref, ng_ref,
              y_ref, xpad_ref, s_ref, vnew_ref, *, T):
    t = pl.program_id(1)

    @pl.when(t == 0)
    def _():
        xpad_ref[0:8, :] = jnp.zeros((8, GDN_CONV_DIM), F32)
        s_ref[...] = jnp.zeros_like(s_ref)

    qkv_raw = main_ref[:, 0:GDN_CONV_DIM]
    z = main_ref[:, GDN_CONV_DIM:GDN_MAIN]
    xa = _silu(_causal_conv(qkv_raw, xpad_ref, cw_ref, T))
    q = xa[:, 0:256]
    k = xa[:, 256:512]
    v = xa[:, 512:768]

    same_head = (_iota((256, 256), 0) // GDN_DK) == (_iota((256, 256), 1) // GDN_DK)
    head_ones = same_head.astype(F32)
    qn = q * lax.rsqrt(_dot_sel_right(q * q, head_ones) + NORM_EPS) * (GDN_DK ** -0.5)
    kn = k * lax.rsqrt(_dot_sel_right(k * k, head_ones) + NORM_EPS)

    sm = small_ref[...]
    beta = jax.nn.sigmoid(sm)
    gl = -jnp.exp(alog_ref[...]) * _softplus(sm + dtb_ref[...])
    ri = _iota((T, T), 0)
    ci = _iota((T, T), 1)
    same_chunk = (ri // CHUNK) == (ci // CHUNK)
    incl = jnp.logical_and(same_chunk, ri >= ci)
    strict = jnp.logical_and(same_chunk, ri > ci)
    gcum = _dot_sel_left(incl, gl)
    er = _iota((LANES, GDN_WIDTH), 0)
    ec = _iota((LANES, GDN_WIDTH), 1) // GDN_DK
    beta_x = _dot_sel_right(beta, er == ec + SMALL_B)
    gcum_x = _dot_sel_right(gcum, er == ec + SMALL_A)
    last_sel = (ci == (ri // CHUNK) * CHUNK + (CHUNK - 1)).astype(F32)
    glast_x = _dot_sel_left(last_sel, gcum_x)
    gcum_t = gcum.T

    kb = kn * beta_x
    eg = jnp.exp(gcum_x)
    rhs = jnp.concatenate([v * beta_x, kb * eg], axis=1).astype(BF16)
    q_dec = (qn * eg).astype(BF16)
    k_dec_t = (kn * jnp.exp(glast_x - gcum_x)).T
    kn_t = kn.T.astype(BF16)
    head_of_lane = _iota((1, GDN_WIDTH), 1) // GDN_DK
    eye = (ri == ci).astype(F32)
    lower_left = []
    b = 1
    while b < CHUNK:
        in_block = (ri // (2 * b)) == (ci // (2 * b))
        lower_left.append(jnp.logical_and(in_block, (ri // b) % 2 > (ci // b) % 2))
        b *= 2

    heads = range(GDN_HEADS)
    hms = [head_of_lane == h for h in heads]
    gammas = [jnp.exp(jnp.where(incl, gcum[:, SMALL_A + h:SMALL_A + h + 1] - gcum_t[SMALL_A + h:SMALL_A + h + 1, :],
                                NEG_BIG)) for h in heads]
    kks = [_dot(jnp.where(hms[h], kb, 0.0).astype(BF16), kn_t) for h in heads]
    mlows = [jnp.where(strict, kks[h] * gammas[h], 0.0) for h in heads]
    ps = [eye - jnp.where(lower_left[0], mlows[h], 0.0) for h in heads]
    for ll in lower_left[1:]:
        pbs = [p.astype(BF16) for p in ps]
        ts = [_dot(pbs[h], jnp.where(ll, mlows[h], 0.0).astype(BF16)).astype(BF16) for h in heads]
        ps = [ps[h] - _dot(ts[h], pbs[h]) for h in heads]
    sols = [_dot(ps[h].astype(BF16), rhs) for h in heads]
    u = sols[0][:, 0:256]
    w = sols[0][:, 256:512]
    for h in range(1, GDN_HEADS):
        u = jnp.where(hms[h], sols[h][:, 0:256], u)
        w = jnp.where(hms[h], sols[h][:, 256:512], w)
    qks = [_dot(jnp.where(hms[h], qn, 0.0).astype(BF16), kn_t) for h in heads]
    qk_heads = [jnp.where(incl, qks[h] * gammas[h], 0.0).astype(BF16) for h in heads]

    vnew_ref[...] = jnp.zeros((T, GDN_WIDTH), F32)
    chunk_of_col = _iota((1, T), 1) // CHUNK
    o_inter = []
    for c in range(T // CHUNK):
        r0 = c * CHUNK
        s_in = s_ref[...]
        s_b = s_in.astype(BF16)
        vnew_ref[r0:r0 + CHUNK, :] = u[r0:r0 + CHUNK, :] - _dot(w[r0:r0 + CHUNK, :].astype(BF16), s_b)
        o_inter.append(_dot(q_dec[r0:r0 + CHUNK, :], s_b))
        kd = jnp.where(chunk_of_col == c, k_dec_t, 0.0).astype(BF16)
        upd = _dot(kd, vnew_ref[...].astype(BF16))
        dec = jnp.exp(glast_x[r0:r0 + 1, :])
        s_ref[...] = jnp.where(same_head, s_in * dec + upd, 0.0)

    vnew = vnew_ref[...].astype(BF16)
    o = jnp.concatenate(o_inter, axis=0)
    for h in range(GDN_HEADS):
        o = o + jnp.where(head_of_lane == h, _dot(qk_heads[h], vnew), 0.0)
    ms = _dot_sel_right(o * o, head_ones) * (1.0 / GDN_DV)
    y = o * lax.rsqrt(ms + NORM_EPS) * ng_ref[...] * _silu(z)
    y_ref[...] = y.astype(BF16)


def _gdn(main, small, cw, dtb, alog, ng, bsz, seq, T):
    T = min(T, seq)
    nt = seq // T
    row = lambda b, t: (b * nt + t, 0)
    fixed = lambda b, t: (0, 0)
    return pl.pallas_call(
        functools.partial(_gdn_body, T=T),
        grid=(bsz, nt),
        in_specs=[
            pl.BlockSpec((T, GDN_MAIN), row),
            pl.BlockSpec((T, LANES), row),
            pl.BlockSpec((CONV_K, GDN_CONV_DIM), fixed),
            pl.BlockSpec((1, LANES), fixed),
            pl.BlockSpec((1, LANES), fixed),
            pl.BlockSpec((1, GDN_WIDTH), fixed),
        ],
        out_specs=pl.BlockSpec((T, GDN_WIDTH), row),
        out_shape=jax.ShapeDtypeStruct((bsz * seq, GDN_WIDTH), BF16),
        scratch_shapes=[
            pltpu.VMEM((8 + T, GDN_CONV_DIM), F32),
            pltpu.VMEM((GDN_WIDTH, GDN_WIDTH), F32),
            pltpu.VMEM((T, GDN_WIDTH), F32),
        ],
        compiler_params=_params(("parallel", "arbitrary")),
        name="gdn",
    )(main, small, cw, dtb, alog, ng)


def _mla_prep_body(c_ref, cos_ref, sin_ref, gq_ref, gkv_ref, wq_ref, wkv_ref, vone_ref,
                   qt_ref, k_ref, vt_ref):
    c = c_ref[...]
    cq = _rms(c[:, 0:256], gq_ref[...]).astype(BF16)
    ckv = _rms(c[:, 256:384], gkv_ref[...]).astype(BF16)
    cos_t = cos_ref[...]
    sin_t = sin_ref[...]
    k_rot = c[:, 384:512] * cos_t + c[:, 512:640] * sin_t
    qq = _dot(cq, wq_ref[...])
    kvv = _dot(ckv, wkv_ref[...])
    width = MLA_HEADS * HEAD_PAD
    qs, ks = [], []
    for h in range(MLA_HEADS):
        lo, hi = h * HEAD_PAD, (h + 1) * HEAD_PAD
        qs.append((qq[:, lo:hi] * cos_t + qq[:, width + lo:width + hi] * sin_t) * MLA_Q_SCALE)
        ks.append(kvv[:, lo:hi] + k_rot)
    qt_ref[0, 0] = jnp.concatenate(qs, axis=1).T.astype(BF16)
    k_ref[...] = jnp.concatenate(ks, axis=1).astype(BF16)
    vt_ref[0, 0] = (kvv[:, width:2 * width] + vone_ref[...]).T.astype(BF16)


def _mla_prep(cols, cos_t, sin_t, gq, gkv, wq, wkv, vone, bsz, seq, tm):
    tm = min(tm, seq)
    nt = seq // tm
    width = MLA_HEADS * HEAD_PAD
    row = lambda i: (i, 0)
    fixed = lambda i: (0, 0)
    return pl.pallas_call(
        _mla_prep_body,
        grid=(bsz * nt,),
        in_specs=[
            pl.BlockSpec((tm, MLA_MAIN), row),
            pl.BlockSpec((tm, LANES), row),
            pl.BlockSpec((tm, LANES), row),
            pl.BlockSpec((1, MLA_Q_LORA), fixed),
            pl.BlockSpec((1, MLA_KV_LORA), fixed),
            pl.BlockSpec((MLA_Q_LORA, 2 * width), fixed),
            pl.BlockSpec((MLA_KV_LORA, 2 * width), fixed),
            pl.BlockSpec((1, width), fixed),
        ],
        out_specs=[
            pl.BlockSpec((1, 1, width, tm), lambda i: (i // nt, i % nt, 0, 0)),
            pl.BlockSpec((tm, width), row),
            pl.BlockSpec((1, 1, width, tm), lambda i: (i // nt, i % nt, 0, 0)),
        ],
        out_shape=[
            jax.ShapeDtypeStruct((bsz, nt, width, tm), BF16),
            jax.ShapeDtypeStruct((bsz * seq, width), BF16),
            jax.ShapeDtypeStruct((bsz, nt, width, tm), BF16),
        ],
        compiler_params=_params(("parallel",)),
        name="mla_prep",
    )(cols, cos_t, sin_t, gq, gkv, wq, wkv, vone)


def _mla_attn_body(qt_ref, k_ref, vt_ref, o_ref, *, tq):
    i = pl.program_id(1)
    visible = (_iota((tq, tq), 0) // CHUNK) <= (_iota((tq, tq), 1) // CHUNK)
    qts = [qt_ref[0, 0, h * HEAD_PAD:(h + 1) * HEAD_PAD, :] for h in range(MLA_HEADS)]

    def step(j, carry, diagonal):
        heads = range(MLA_HEADS)
        kblk = k_ref[0, pl.ds(pl.multiple_of(j * tq, tq), tq), :]
        ss = [_dot(kblk[:, h * HEAD_PAD:(h + 1) * HEAD_PAD], qts[h]) for h in heads]
        if diagonal:
            ss = [jnp.where(visible, s, NEG_BIG) for s in ss]
        m_new = [jnp.maximum(carry[h][0], jnp.max(ss[h], axis=0, keepdims=True)) for h in heads]
        ps = [jnp.exp2(ss[h] - m_new[h]).astype(BF16) for h in heads]
        pv = [_dot(vt_ref[0, j, h * HEAD_PAD:(h + 1) * HEAD_PAD, :], ps[h]) for h in heads]
        return tuple((m_new[h], jnp.exp2(carry[h][0] - m_new[h]) * carry[h][1] + pv[h]) for h in heads)

    init = tuple((jnp.full((1, tq), NEG_BIG, F32), jnp.zeros((HEAD_PAD, tq), F32)) for _ in range(MLA_HEADS))
    carry = lax.fori_loop(0, i, functools.partial(step, diagonal=False), init)
    carry = step(i, carry, True)
    outs = []
    for h in range(MLA_HEADS):
        acc = carry[h][1]
        outs.append((acc / acc[MLA_V:MLA_V + 1, :]).T)
    o_ref[...] = jnp.concatenate(outs, axis=1).astype(BF16)


def _mla_attn(qt, k, vt, bsz, seq, tq):
    tq = min(tq, seq)
    nq = seq // tq
    width = MLA_HEADS * HEAD_PAD
    return pl.pallas_call(
        functools.partial(_mla_attn_body, tq=tq),
        grid=(bsz, nq),
        in_specs=[
            pl.BlockSpec((1, 1, width, tq), lambda b, i: (b, i, 0, 0)),
            pl.BlockSpec((1, seq, width), lambda b, i: (b, 0, 0)),
            pl.BlockSpec((1, nq, width, tq), lambda b, i: (b, 0, 0, 0)),
        ],
        out_specs=pl.BlockSpec((tq, width), lambda b, i: (b * nq + i, 0)),
        out_shape=jax.ShapeDtypeStruct((bsz * seq, width), BF16),
        compiler_params=_params(("parallel", "arbitrary")),
        name="mla_attn",
    )(qt, k.reshape(bsz, seq, width), vt)


def _memkv_body(mem_ref, g_ref, w_ref, kt_ref, v_ref):
    mn = _rms(mem_ref[...], g_ref[...]).astype(BF16)
    kv = _dot(mn, w_ref[...])
    kt_ref[0] = kv[:, 0:D_MODEL].T.astype(BF16)
    v_ref[0] = kv[:, D_MODEL:2 * D_MODEL].astype(BF16)


def _memkv(mem2d, g, w, bsz):
    return pl.pallas_call(
        _memkv_body,
        grid=(bsz,),
        in_specs=[
            pl.BlockSpec((N_MEM, D_MODEL), lambda b: (b, 0)),
            pl.BlockSpec((1, D_MODEL), lambda b: (0, 0)),
            pl.BlockSpec((D_MODEL, 2 * D_MODEL), lambda b: (0, 0)),
        ],
        out_specs=[
            pl.BlockSpec((1, D_MODEL, N_MEM), lambda b: (b, 0, 0)),
            pl.BlockSpec((1, N_MEM, D_MODEL), lambda b: (b, 0, 0)),
        ],
        out_shape=[
            jax.ShapeDtypeStruct((bsz, D_MODEL, N_MEM), BF16),
            jax.ShapeDtypeStruct((bsz, N_MEM, D_MODEL), BF16),
        ],
        compiler_params=_params(("parallel",)),
        name="memkv",
    )(mem2d, g, w)


def _post_body(h_ref, ys_ref, ym_ref, yg_ref, wos_ref, wom_ref, wog_ref, gmix_ref, gxa_ref,
               wq_ref, kt_ref, v_ref, wo_ref, gpost_ref, o_ref):
    mix = _dot(ys_ref[...], wos_ref[...]) + _dot(ym_ref[...], wom_ref[...]) + _dot(yg_ref[...], wog_ref[...])
    h1 = h_ref[...] + _rms(mix, gmix_ref[...])
    u = _rms(h1, gxa_ref[...]).astype(BF16)
    q = (_dot(u, wq_ref[...]) * (XA_HEAD_DIM ** -0.5)).astype(BF16)
    heads = range(XA_HEADS)
    w = XA_HEAD_DIM
    ss = [_dot(q[:, hd * w:(hd + 1) * w], kt_ref[0, hd * w:(hd + 1) * w, :]) for hd in heads]
    ps = [jnp.exp(s - jnp.max(s, axis=-1, keepdims=True)) for s in ss]
    os_ = [_dot(ps[hd].astype(BF16), v_ref[0, :, hd * w:(hd + 1) * w]) for hd in heads]
    outs = [(os_[hd] / jnp.sum(ps[hd], axis=-1, keepdims=True)).astype(BF16) for hd in heads]
    xa = _dot(jnp.concatenate(outs, axis=1), wo_ref[...])
    o_ref[...] = h1 + _rms(xa, gpost_ref[...])


def _post(h, ys, ym, yg, wos, wom, wog, gmix, gxa, wq, kt, v, wo, gpost, bsz, seq, tm):
    tm = min(tm, seq)
    nt = seq // tm
    row = lambda b, t: (b * nt + t, 0)
    fixed = lambda b, t: (0, 0)
    full = lambda a: pl.BlockSpec(a.shape, fixed)
    return pl.pallas_call(
        _post_body,
        grid=(bsz, nt),
        in_specs=[
            pl.BlockSpec((tm, D_MODEL), row),
            pl.BlockSpec((tm, ys.shape[1]), row),
            pl.BlockSpec((tm, ym.shape[1]), row),
            pl.BlockSpec((tm, yg.shape[1]), row),
            full(wos), full(wom), full(wog), full(gmix), full(gxa), full(wq),
            pl.BlockSpec((1, D_MODEL, N_MEM), lambda b, t: (b, 0, 0)),
            pl.BlockSpec((1, N_MEM, D_MODEL), lambda b, t: (b, 0, 0)),
            full(wo), full(gpost),
        ],
        out_specs=pl.BlockSpec((tm, D_MODEL), row),
        out_shape=jax.ShapeDtypeStruct((bsz * seq, D_MODEL), F32),
        compiler_params=_params(("parallel", "arbitrary")),
        name="mix_out_xattn",
    )(h, ys, ym, yg, wos, wom, wog, gmix, gxa, wq, kt, v, wo, gpost)


def _pad_lanes(vals, offset, width=LANES):
    n = vals.shape[-1]
    pad = [(0, 0)] * (vals.ndim - 1) + [(offset, width - offset - n)]
    return jnp.pad(vals, pad)[..., None, :]


def _prep_params(norm_g, ffn_w_up, ffn_w_down, w_in, ssd_conv_w, ssd_conv_b, ssd_dt_bias, ssd_a_log,
                 ssd_d, ssd_norm_g, mla_q_norm_g, mla_w_uq, mla_kv_norm_g, mla_w_ukv, gdn_conv_w,
                 gdn_dt_bias, gdn_a_log, gdn_norm_g, w_out, xa_w_q, xa_w_kv, xa_w_o, tf):
    nj = D_FF // tf
    depth = norm_g.shape[0]
    gate = ffn_w_up[..., :D_FF].reshape(depth, 2, D_MODEL, nj, 1, tf)
    up = ffn_w_up[..., D_FF:].reshape(depth, 2, D_MODEL, nj, 1, tf)
    wgu = jnp.concatenate([gate, up], axis=4).astype(BF16)
    wgu = jnp.transpose(wgu, (0, 1, 3, 2, 4, 5)).reshape(depth, 2, nj, D_MODEL, 2 * tf)
    vone = np.zeros((1, MLA_HEADS * HEAD_PAD), np.float32)
    vone[0, np.arange(MLA_HEADS) * HEAD_PAD + MLA_V] = 1.0
    return dict(
        norm_g=norm_g[:, :, None, :],
        wgu=wgu,
        wd=ffn_w_down.astype(BF16).reshape(depth, 2, nj, tf, D_MODEL),
        w_in=_gather_cols(w_in, _IN_PERM).astype(BF16),
        ssd_cw=ssd_conv_w,
        ssd_cb=ssd_conv_b[:, None, :],
        ssd_dtb=_pad_lanes(ssd_dt_bias, SMALL_DT),
        ssd_alog=_pad_lanes(ssd_a_log, SMALL_DT),
        ssd_dsk=jnp.repeat(ssd_d, SSD_HEAD_DIM, axis=-1)[:, None, :],
        ssd_ng=ssd_norm_g[:, None, :],
        mla_gq=mla_q_norm_g[:, None, :],
        mla_gkv=mla_kv_norm_g[:, None, :],
        mla_wq=_gather_cols(mla_w_uq, _UQ_PERM).astype(BF16),
        mla_wkv=_gather_cols(mla_w_ukv, _UKV_PERM).astype(BF16),
        mla_vone=jnp.broadcast_to(jnp.asarray(vone), (depth,) + vone.shape),
        gdn_cw=gdn_conv_w,
        gdn_dtb=_pad_lanes(gdn_dt_bias, SMALL_A),
        gdn_alog=_pad_lanes(gdn_a_log, SMALL_A),
        gdn_ng=jnp.tile(gdn_norm_g, (1, GDN_HEADS))[:, None, :],
        wo_ssd=w_out[:, 0:SSD_WIDTH, :].astype(BF16),
        wo_mla=jnp.swapaxes(_gather_cols(jnp.swapaxes(w_out, 1, 2), _WOUT_MLA_ROWS), 1, 2).astype(BF16),
        wo_gdn=w_out[:, SSD_WIDTH + MLA_HEADS * MLA_V:, :].astype(BF16),
        xa_wq=xa_w_q.astype(BF16),
        xa_wkv=xa_w_kv.astype(BF16),
        xa_wo=xa_w_o.astype(BF16),
    )


def _rope_tables(positions):
    inv = 1.0 / (ROPE_THETA ** (jnp.arange(0, MLA_ROPE, 2, dtype=F32) / MLA_ROPE))
    ang = positions.astype(F32).reshape(-1, 1) * inv
    cos, sin = jnp.cos(ang), jnp.sin(ang)
    n = ang.shape[0]
    ones = jnp.ones((n, MLA_NOPE), F32)
    zeros_nope = jnp.zeros((n, MLA_NOPE), F32)
    zeros_pad = jnp.zeros((n, HEAD_PAD - MLA_NOPE - MLA_ROPE), F32)
    cos_t = jnp.concatenate([ones, cos, cos, zeros_pad], axis=1)
    sin_t = jnp.concatenate([zeros_nope, -sin, sin, zeros_pad], axis=1)
    return cos_t, sin_t


FFN_TM = 512
FFN_TF = 1408
TOKEN_TM = 512
SEQ_TILE = 256


def kernel(x, mem, positions, norm_g, ffn_w_up, ffn_w_down, w_in, ssd_conv_w, ssd_conv_b, ssd_dt_bias,
           ssd_a_log, ssd_d, ssd_norm_g, mla_q_norm_g, mla_w_uq, mla_kv_norm_g, mla_w_ukv, gdn_conv_w,
           gdn_dt_bias, gdn_a_log, gdn_norm_g, w_out, xa_w_q, xa_w_kv, xa_w_o):
    bsz, seq, _ = x.shape
    n = bsz * seq
    cos_t, sin_t = _rope_tables(positions)
    params = _prep_params(norm_g, ffn_w_up, ffn_w_down, w_in, ssd_conv_w, ssd_conv_b, ssd_dt_bias,
                          ssd_a_log, ssd_d, ssd_norm_g, mla_q_norm_g, mla_w_uq, mla_kv_norm_g, mla_w_ukv,
                          gdn_conv_w, gdn_dt_bias, gdn_a_log, gdn_norm_g, w_out, xa_w_q, xa_w_kv, xa_w_o,
                          FFN_TF)
    mem2d = mem.reshape(bsz * N_MEM, D_MODEL)

    def layer(h, p):
        g = p["norm_g"]
        h = _ffn(h, g[FFN1_PRE], p["wgu"][0], p["wd"][0], g[FFN1_POST], FFN_TM)
        c_ssd, c_mla, c_gdn, c_small = _inproj(h, g[MIX_PRE], p["w_in"], TOKEN_TM)
        y_ssd = _ssd(c_ssd, c_small, p["ssd_cw"], p["ssd_cb"], p["ssd_dtb"], p["ssd_alog"], p["ssd_dsk"],
                     p["ssd_ng"], bsz, seq, SEQ_TILE)
        qt, kk, vt = _mla_prep(c_mla, cos_t, sin_t, p["mla_gq"], p["mla_gkv"], p["mla_wq"], p["mla_wkv"],
                               p["mla_vone"], bsz, seq, SEQ_TILE)
        y_mla = _mla_attn(qt, kk, vt, bsz, seq, SEQ_TILE)
        y_gdn = _gdn(c_gdn, c_small, p["gdn_cw"], p["gdn_dtb"], p["gdn_alog"], p["gdn_ng"], bsz, seq, SEQ_TILE)
        mkt, mv = _memkv(mem2d, g[MEM_NORM], p["xa_wkv"], bsz)
        h = _post(h, y_ssd, y_mla, y_gdn, p["wo_ssd"], p["wo_mla"], p["wo_gdn"], g[MIX_POST], g[XA_PRE],
                  p["xa_wq"], mkt, mv, p["xa_wo"], g[XA_POST], bsz, seq, TOKEN_TM)
        h = _ffn(h, g[FFN2_PRE], p["wgu"][1], p["wd"][1], g[FFN2_POST], FFN_TM)
        return h, None

    h, _ = lax.scan(layer, x.reshape(n, D_MODEL), params)
    return h.reshape(bsz, seq, D_MODEL)
```

```python
import functools

import numpy as np
import jax
import jax.numpy as jnp
from jax import lax
from jax.experimental import pallas as pl
from jax.experimental.pallas import tpu as pltpu

F32 = jnp.float32
BF16 = jnp.bfloat16

D_MODEL = 1024
DEPTH = 4
CHUNK = 64
NORM_EPS = 1e-6
CONV_K = 4
D_FF = 2816
N_MEM = 256

SSD_HEADS = 8
SSD_HEAD_DIM = 64
SSD_WIDTH = 512
SSD_GROUPS = 2
SSD_STATE = 64
SSD_CONV_DIM = 768
SSD_IN = 1288

MLA_HEADS = 4
MLA_Q_LORA = 256
MLA_KV_LORA = 128
MLA_NOPE = 64
MLA_ROPE = 32
MLA_V = 64
MLA_IN = 416
ROPE_THETA = 10000.0
MLA_SCALE = float((MLA_NOPE + MLA_ROPE) ** -0.5)
MLA_Q_SCALE = MLA_SCALE * float(np.log2(np.e))

GDN_HEADS = 4
GDN_DK = 64
GDN_DV = 64
GDN_CONV_DIM = 768
GDN_WIDTH = 256
GDN_IN = 1032

XA_HEADS = 4
XA_HEAD_DIM = 256

(FFN1_PRE, FFN1_POST, MIX_PRE, MIX_POST, MEM_NORM, XA_PRE, XA_POST, FFN2_PRE, FFN2_POST) = range(9)

LANES = 128
HEAD_PAD = 128
NEG_BIG = -1e30
VMEM_LIMIT = 56 * 1024 * 1024

IN_COLS_PADDED = 3072
SSD_MAIN = 1280
MLA_MAIN = 640
GDN_MAIN = 1024
SMALL_DT, SMALL_B, SMALL_A = 0, 8, 12


def _in_perm():
    perm = -np.ones((IN_COLS_PADDED,), np.int64)
    perm[0:1280] = np.arange(0, 1280)
    mla0 = SSD_IN
    perm[1280:1664] = np.arange(mla0, mla0 + 384)
    kr0 = mla0 + 384
    half = MLA_ROPE // 2
    perm[1664 + 64:1664 + 96] = np.arange(kr0, kr0 + 32)
    perm[1792 + 64:1792 + 64 + half] = np.arange(kr0 + half, kr0 + 32)
    perm[1792 + 64 + half:1792 + 96] = np.arange(kr0, kr0 + half)
    gdn0 = SSD_IN + MLA_IN
    perm[1920:2944] = np.arange(gdn0, gdn0 + 1024)
    perm[2944 + SMALL_DT:2944 + SMALL_DT + 8] = np.arange(1280, 1288)
    perm[2944 + SMALL_B:2944 + SMALL_B + 4] = np.arange(gdn0 + 1024, gdn0 + 1028)
    perm[2944 + SMALL_A:2944 + SMALL_A + 4] = np.arange(gdn0 + 1028, gdn0 + 1032)
    return perm


_IN_PERM = _in_perm()


def _gather_cols(w, perm):
    valid = jnp.asarray(perm >= 0)
    idx = jnp.asarray(np.maximum(perm, 0))
    return jnp.where(valid, jnp.take(w, idx, axis=-1), 0.0)


def _uq_perms():
    a = -np.ones((MLA_HEADS * HEAD_PAD,), np.int64)
    b = -np.ones((MLA_HEADS * HEAD_PAD,), np.int64)
    hd = MLA_NOPE + MLA_ROPE
    half = MLA_ROPE // 2
    for h in range(MLA_HEADS):
        a[h * HEAD_PAD:h * HEAD_PAD + hd] = np.arange(h * hd, (h + 1) * hd)
        r0 = h * hd + MLA_NOPE
        b[h * HEAD_PAD + 64:h * HEAD_PAD + 64 + half] = np.arange(r0 + half, r0 + 32)
        b[h * HEAD_PAD + 64 + half:h * HEAD_PAD + 96] = np.arange(r0, r0 + half)
    return np.concatenate([a, b])


def _ukv_perms():
    k = -np.ones((MLA_HEADS * HEAD_PAD,), np.int64)
    v = -np.ones((MLA_HEADS * HEAD_PAD,), np.int64)
    hd = MLA_NOPE + MLA_V
    for h in range(MLA_HEADS):
        k[h * HEAD_PAD:h * HEAD_PAD + MLA_NOPE] = np.arange(h * hd, h * hd + MLA_NOPE)
        v[h * HEAD_PAD:h * HEAD_PAD + MLA_V] = np.arange(h * hd + MLA_NOPE, (h + 1) * hd)
    return np.concatenate([k, v])


_UQ_PERM = _uq_perms()
_UKV_PERM = _ukv_perms()


def _wout_mla_rows():
    rows = -np.ones((MLA_HEADS * HEAD_PAD,), np.int64)
    for h in range(MLA_HEADS):
        rows[h * HEAD_PAD:h * HEAD_PAD + MLA_V] = SSD_WIDTH + np.arange(h * MLA_V, (h + 1) * MLA_V)
    return rows


_WOUT_MLA_ROWS = _wout_mla_rows()


def _rms(x, g):
    return x * lax.rsqrt(jnp.mean(x * x, axis=-1, keepdims=True) + NORM_EPS) * g


def _silu(x):
    return x * jax.nn.sigmoid(x)


def _softplus(x):
    return jnp.maximum(x, 0.0) + jnp.log1p(jnp.exp(-jnp.abs(x)))


def _dot(a, b):
    return jnp.dot(a, b, preferred_element_type=F32)


def _split3(x):
    hi = x.astype(BF16)
    r1 = x - hi.astype(F32)
    mid = r1.astype(BF16)
    lo = (r1 - mid.astype(F32)).astype(BF16)
    return hi, mid, lo


def _dot_sel_right(x, sel):
    sel = sel.astype(BF16)
    hi, mid, lo = _split3(x)
    return _dot(hi, sel) + (_dot(mid, sel) + _dot(lo, sel))


def _dot_sel_left(sel, x):
    sel = sel.astype(BF16)
    hi, mid, lo = _split3(x)
    return _dot(sel, hi) + (_dot(sel, mid) + _dot(sel, lo))


def _iota(shape, dim):
    return lax.broadcasted_iota(jnp.int32, shape, dim)


def _params(sem):
    return pltpu.CompilerParams(dimension_semantics=sem, vmem_limit_bytes=VMEM_LIMIT)


def _ffn_body(h_ref, gpre_ref, wg_ref, wu_ref, wd_ref, gpost_ref, o_ref, xn_ref, acc_ref, *, nj):
    j = pl.program_id(1)

    @pl.when(j == 0)
    def _():
        xn_ref[...] = _rms(h_ref[...], gpre_ref[...]).astype(BF16)

    xn = xn_ref[...]
    act = (_silu(_dot(xn, wg_ref[...])) * _dot(xn, wu_ref[...])).astype(BF16)
    part = _dot(act, wd_ref[0])

    @pl.when(j == 0)
    def _():
        acc_ref[...] = part

    @pl.when(j > 0)
    def _():
        acc_ref[...] += part

    @pl.when(j == nj - 1)
    def _():
        o_ref[...] = h_ref[...] + 0.5 * _rms(acc_ref[...], gpost_ref[...])


def _ffn(h, g_pre, w_up, wd, g_post, tm):
    n = h.shape[0]
    nj, tf, _ = wd.shape
    tm = min(tm, n)
    return pl.pallas_call(
        functools.partial(_ffn_body, nj=nj),
        grid=(n // tm, nj),
        in_specs=[
            pl.BlockSpec((tm, D_MODEL), lambda i, j: (i, 0)),
            pl.BlockSpec((1, D_MODEL), lambda i, j: (0, 0)),
            pl.BlockSpec((D_MODEL, tf), lambda i, j: (0, j)),
            pl.BlockSpec((D_MODEL, tf), lambda i, j: (0, nj + j)),
            pl.BlockSpec((1, tf, D_MODEL), lambda i, j: (j, 0, 0)),
            pl.BlockSpec((1, D_MODEL), lambda i, j: (0, 0)),
        ],
        out_specs=pl.BlockSpec((tm, D_MODEL), lambda i, j: (i, 0)),
        out_shape=jax.ShapeDtypeStruct((n, D_MODEL), F32),
        scratch_shapes=[pltpu.VMEM((tm, D_MODEL), BF16), pltpu.VMEM((tm, D_MODEL), F32)],
        compiler_params=_params(("parallel", "arbitrary")),
        name="ffn",
    )(h, g_pre, w_up, w_up, wd, g_post)


def _ffn_resident_body(h_ref, gpre_ref, wg_ref, wu_ref, wd_ref, gpost_ref, o_ref):
    h = h_ref[...]
    xn = _rms(h, gpre_ref[...]).astype(BF16)
    act = (_silu(_dot(xn, wg_ref[...])) * _dot(xn, wu_ref[...])).astype(BF16)
    o_ref[...] = h + 0.5 * _rms(_dot(act, wd_ref[...]), gpost_ref[...])


def _ffn_resident(h, g_pre, w_up, w_down, g_post, tm):
    n = h.shape[0]
    tm = min(tm, n)
    resident = pl.Buffered(1)
    return pl.pallas_call(
        _ffn_resident_body,
        grid=(n // tm,),
        in_specs=[
            pl.BlockSpec((tm, D_MODEL), lambda i: (i, 0)),
            pl.BlockSpec((1, D_MODEL), lambda i: (0, 0)),
            pl.BlockSpec((D_MODEL, D_FF), lambda i: (0, 0), pipeline_mode=resident),
            pl.BlockSpec((D_MODEL, D_FF), lambda i: (0, 1), pipeline_mode=resident),
            pl.BlockSpec((D_FF, D_MODEL), lambda i: (0, 0), pipeline_mode=resident),
            pl.BlockSpec((1, D_MODEL), lambda i: (0, 0)),
        ],
        out_specs=pl.BlockSpec((tm, D_MODEL), lambda i: (i, 0)),
        out_shape=jax.ShapeDtypeStruct((n, D_MODEL), F32),
        compiler_params=_params(("parallel",)),
        name="ffn_resident",
    )(h, g_pre, w_up, w_up, w_down, g_post)


def _inproj_body(h_ref, g_ref, w_ref, ssd_ref, mla_ref, gdn_ref, small_ref):
    xn = _rms(h_ref[...], g_ref[...]).astype(BF16)
    y = _dot(xn, w_ref[...])
    ssd_ref[...] = y[:, 0:1280]
    mla_ref[...] = y[:, 1280:1920]
    gdn_ref[...] = y[:, 1920:2944]
    small_ref[...] = y[:, 2944:3072]


def _inproj(h, g, w, tm):
    n = h.shape[0]
    tm = min(tm, n)
    row = lambda i: (i, 0)
    fixed = lambda i: (0, 0)
    return pl.pallas_call(
        _inproj_body,
        grid=(n // tm,),
        in_specs=[
            pl.BlockSpec((tm, D_MODEL), row),
            pl.BlockSpec((1, D_MODEL), fixed),
            pl.BlockSpec((D_MODEL, IN_COLS_PADDED), fixed),
        ],
        out_specs=[
            pl.BlockSpec((tm, SSD_MAIN), row),
            pl.BlockSpec((tm, MLA_MAIN), row),
            pl.BlockSpec((tm, GDN_MAIN), row),
            pl.BlockSpec((tm, LANES), row),
        ],
        out_shape=[
            jax.ShapeDtypeStruct((n, SSD_MAIN), F32),
            jax.ShapeDtypeStruct((n, MLA_MAIN), F32),
            jax.ShapeDtypeStruct((n, GDN_MAIN), F32),
            jax.ShapeDtypeStruct((n, LANES), F32),
        ],
        compiler_params=_params(("parallel",)),
        name="inproj",
    )(h, g, w)


def _causal_conv(x, xpad_ref, cw_ref, tile):
    xpad_ref[8:8 + tile, :] = x
    acc = cw_ref[CONV_K - 1:CONV_K, :] * x
    for k in range(CONV_K - 1):
        off = 8 - (CONV_K - 1) + k
        acc = acc + cw_ref[k:k + 1, :] * xpad_ref[off:off + tile, :]
    xpad_ref[0:8, :] = x[tile - 8:tile, :]
    return acc


def _ssd_body(main_ref, small_ref, cw_ref, cb_ref, dtb_ref, alog_ref, dsk_ref, ng_ref,
              y_ref, xpad_ref, st_ref, *, L):
    t = pl.program_id(1)

    @pl.when(t == 0)
    def _():
        xpad_ref[0:8, :] = jnp.zeros((8, SSD_CONV_DIM), F32)
        st_ref[...] = jnp.zeros_like(st_ref)

    z = main_ref[:, 0:SSD_WIDTH]
    xbc = main_ref[:, SSD_WIDTH:SSD_MAIN]
    xa = _silu(_causal_conv(xbc, xpad_ref, cw_ref, L) + cb_ref[...])
    xs = xa[:, 0:512]
    bm = xa[:, 512:640]
    cm = xa[:, 640:768]

    dt = _softplus(small_ref[...] + dtb_ref[...])
    dta = dt * (-jnp.exp(alog_ref[...]))
    ri = _iota((L, L), 0)
    ci = _iota((L, L), 1)
    causal = ri >= ci
    acum = _dot_sel_left(causal, dta)
    expand = (_iota((LANES, SSD_WIDTH), 0) == _iota((LANES, SSD_WIDTH), 1) // SSD_HEAD_DIM).astype(F32)
    dt_x = _dot_sel_right(dt, expand)
    acum_x = _dot_sel_right(acum, expand)
    acum_t = acum.T
    bm_t = bm.T.astype(BF16)
    alast_x = acum_x[L - 1:L, :]
    xdt = xs * dt_x

    state_in = st_ref[...]
    y_off = _dot(cm.astype(BF16), state_in.astype(BF16)) * jnp.exp(acum_x)
    w_end = (xdt * jnp.exp(alast_x - acum_x)).astype(BF16)
    chunk_state = _dot(bm_t, w_end)
    own_group = (_iota((LANES, SSD_WIDTH), 0) // SSD_STATE) == (_iota((LANES, SSD_WIDTH), 1) // 256)
    st_ref[...] = jnp.where(own_group, state_in * jnp.exp(alast_x) + chunk_state, 0.0)

    lane = _iota((L, LANES), 1)
    group_of_lane = _iota((1, LANES), 1) // SSD_STATE
    heads = range(SSD_HEADS)
    hpg = SSD_HEADS // SSD_GROUPS
    cbs = [_dot(jnp.where(group_of_lane == g, cm, 0.0).astype(BF16), bm_t) for g in range(SSD_GROUPS)]
    segs = [jnp.where(causal, acum[:, hh:hh + 1] - acum_t[hh:hh + 1, :], NEG_BIG) for hh in heads]
    gs = [(cbs[hh // hpg] * jnp.exp(segs[hh])).astype(BF16) for hh in heads]
    xps = [xdt[:, slot * LANES:(slot + 1) * LANES].astype(BF16) for slot in range(SSD_HEADS // 2)]
    res = [_dot(gs[hh], xps[hh // 2]) for hh in heads]
    parts = [jnp.where(lane < SSD_HEAD_DIM, res[2 * slot], res[2 * slot + 1]) for slot in range(SSD_HEADS // 2)]
    y = jnp.concatenate(parts, axis=1) + y_off + xs * dsk_ref[...]
    y = y * _silu(z)
    outs = []
    for g in range(SSD_GROUPS):
        yg = y[:, g * 256:(g + 1) * 256]
        outs.append(yg * lax.rsqrt(jnp.mean(yg * yg, axis=-1, keepdims=True) + NORM_EPS))
    y_ref[...] = (jnp.concatenate(outs, axis=1) * ng_ref[...]).astype(BF16)


def _ssd(main, small, cw, cb, dtb, alog, dsk, ng, bsz, seq, L):
    L = min(L, seq)
    nt = seq // L
    row = lambda b, t: (b * nt + t, 0)
    fixed = lambda b, t: (0, 0)
    return pl.pallas_call(
        functools.partial(_ssd_body, L=L),
        grid=(bsz, nt),
        in_specs=[
            pl.BlockSpec((L, SSD_MAIN), row),
            pl.BlockSpec((L, LANES), row),
            pl.BlockSpec((CONV_K, SSD_CONV_DIM), fixed),
            pl.BlockSpec((1, SSD_CONV_DIM), fixed),
            pl.BlockSpec((1, LANES), fixed),
            pl.BlockSpec((1, LANES), fixed),
            pl.BlockSpec((1, SSD_WIDTH), fixed),
            pl.BlockSpec((1, SSD_WIDTH), fixed),
        ],
        out_specs=pl.BlockSpec((L, SSD_WIDTH), row),
        out_shape=jax.ShapeDtypeStruct((bsz * seq, SSD_WIDTH), BF16),
        scratch_shapes=[pltpu.VMEM((8 + L, SSD_CONV_DIM), F32), pltpu.VMEM((LANES, SSD_WIDTH), F32)],
        compiler_params=_params(("parallel", "arbitrary")),
        name="ssd",
    )(main, small, cw, cb, dtb, alog, dsk, ng)


def _gdn_body(main_ref, small_ref, cw_ref, dtb_ref, alog_ref, ng_ref,
              y_ref, xpad_ref, s_ref, vnew_ref, *, T):
    t = pl.program_id(1)

    @pl.when(t == 0)
    def _():
        xpad_ref[0:8, :] = jnp.zeros((8, GDN_CONV_DIM), F32)
        s_ref[...] = jnp.zeros_like(s_ref)

    qkv_raw = main_ref[:, 0:GDN_CONV_DIM]
    z = main_ref[:, GDN_CONV_DIM:GDN_MAIN]
    xa = _silu(_causal_conv(qkv_raw, xpad_ref, cw_ref, T))
    q = xa[:, 0:256]
    k = xa[:, 256:512]
    v = xa[:, 512:768]

    same_head = (_iota((256, 256), 0) // GDN_DK) == (_iota((256, 256), 1) // GDN_DK)
    head_ones = same_head.astype(F32)
    qn = q * lax.rsqrt(_dot_sel_right(q * q, head_ones) + NORM_EPS) * (GDN_DK ** -0.5)
    kn = k * lax.rsqrt(_dot_sel_right(k * k, head_ones) + NORM_EPS)

    sm = small_ref[...]
    beta = jax.nn.sigmoid(sm)
    gl = -jnp.exp(alog_ref[...]) * _softplus(sm + dtb_ref[...])
    ri = _iota((T, T), 0)
    ci = _iota((T, T), 1)
    same_chunk = (ri // CHUNK) == (ci // CHUNK)
    incl = jnp.logical_and(same_chunk, ri >= ci)
    strict = jnp.logical_and(same_chunk, ri > ci)
    gcum = _dot_sel_left(incl, gl)
    er = _iota((LANES, GDN_WIDTH), 0)
    ec = _iota((LANES, GDN_WIDTH), 1) // GDN_DK
    beta_x = _dot_sel_right(beta, er == ec + SMALL_B)
    gcum_x = _dot_sel_right(gcum, er == ec + SMALL_A)
    last_sel = (ci == (ri // CHUNK) * CHUNK + (CHUNK - 1)).astype(F32)
    glast_x = _dot_sel_left(last_sel, gcum_x)
    gcum_t = gcum.T

    kb = kn * beta_x
    eg = jnp.exp(gcum_x)
    rhs = jnp.concatenate([v * beta_x, kb * eg], axis=1).astype(BF16)
    q_dec = (qn * eg).astype(BF16)
    k_dec_t = (kn * jnp.exp(glast_x - gcum_x)).T
    kn_t = kn.T.astype(BF16)
    head_of_lane = _iota((1, GDN_WIDTH), 1) // GDN_DK
    eye = (ri == ci).astype(F32)
    lower_left = []
    b = 1
    while b < CHUNK:
        in_block = (ri // (2 * b)) == (ci // (2 * b))
        lower_left.append(jnp.logical_and(in_block, (ri // b) % 2 > (ci // b) % 2))
        b *= 2

    heads = range(GDN_HEADS)
    hms = [head_of_lane == h for h in heads]
    gammas = [jnp.exp(jnp.where(incl, gcum[:, SMALL_A + h:SMALL_A + h + 1] - gcum_t[SMALL_A + h:SMALL_A + h + 1, :],
                                NEG_BIG)) for h in heads]
    kks = [_dot(jnp.where(hms[h], kb, 0.0).astype(BF16), kn_t) for h in heads]
    mlows = [jnp.where(strict, kks[h] * gammas[h], 0.0) for h in heads]
    ps = [eye - jnp.where(lower_left[0], mlows[h], 0.0) for h in heads]
    for ll in lower_left[1:]:
        pbs = [p.astype(BF16) for p in ps]
        ts = [_dot(pbs[h], jnp.where(ll, mlows[h], 0.0).astype(BF16)).astype(BF16) for h in heads]
        ps = [ps[h] - _dot(ts[h], pbs[h]) for h in heads]
    sols = [_dot(ps[h].astype(BF16), rhs) for h in heads]
    u = sols[0][:, 0:256]
    w = sols[0][:, 256:512]
    for h in range(1, GDN_HEADS):
        u = jnp.where(hms[h], sols[h][:, 0:256], u)
        w = jnp.where(hms[h], sols[h][:, 256:512], w)
    qks = [_dot(jnp.where(hms[h], qn, 0.0).astype(BF16), kn_t) for h in heads]
    qk_heads = [jnp.where(incl, qks[h] * gammas[h], 0.0).astype(BF16) for h in heads]

    vnew_ref[...] = jnp.zeros((T, GDN_WIDTH), F32)
    chunk_of_col = _iota((1, T), 1) // CHUNK
    o_inter = []
    for c in range(T // CHUNK):
        r0 = c * CHUNK
        s_in = s_ref[...]
        s_b = s_in.astype(BF16)
        vnew_ref[r0:r0 + CHUNK, :] = u[r0:r0 + CHUNK, :] - _dot(w[r0:r0 + CHUNK, :].astype(BF16), s_b)
        o_inter.append(_dot(q_dec[r0:r0 + CHUNK, :], s_b))
        kd = jnp.where(chunk_of_col == c, k_dec_t, 0.0).astype(BF16)
        upd = _dot(kd, vnew_ref[...].astype(BF16))
        dec = jnp.exp(glast_x[r0:r0 + 1, :])
        s_ref[...] = jnp.where(same_head, s_in * dec + upd, 0.0)

    vnew = vnew_ref[...].astype(BF16)
    o = jnp.concatenate(o_inter, axis=0)
    for h in range(GDN_HEADS):
        o = o + jnp.where(head_of_lane == h, _dot(qk_heads[h], vnew), 0.0)
    ms = _dot_sel_right(o * o, head_ones) * (1.0 / GDN_DV)
    y = o * lax.rsqrt(ms + NORM_EPS) * ng_ref[...] * _silu(z)
    y_ref[...] = y.astype(BF16)


def _gdn(main, small, cw, dtb, alog, ng, bsz, seq, T):
    T = min(T, seq)
    nt = seq // T
    row = lambda b, t: (b * nt + t, 0)
    fixed = lambda b, t: (0, 0)
    return pl.pallas_call(
        functools.partial(_gdn_body, T=T),
        grid=(bsz, nt),
        in_specs=[
            pl.BlockSpec((T, GDN_MAIN), row),
            pl.BlockSpec((T, LANES), row),
            pl.BlockSpec((CONV_K, GDN_CONV_DIM), fixed),
            pl.BlockSpec((1, LANES), fixed),
            pl.BlockSpec((1, LANES), fixed),
            pl.BlockSpec((1, GDN_WIDTH), fixed),
        ],
        out_specs=pl.BlockSpec((T, GDN_WIDTH), row),
        out_shape=jax.ShapeDtypeStruct((bsz * seq, GDN_WIDTH), BF16),
        scratch_shapes=[
            pltpu.VMEM((8 + T, GDN_CONV_DIM), F32),
            pltpu.VMEM((GDN_WIDTH, GDN_WIDTH), F32),
            pltpu.VMEM((T, GDN_WIDTH), F32),
        ],
        compiler_params=_params(("parallel", "arbitrary")),
        name="gdn",
    )(main, small, cw, dtb, alog, ng)


def _mla_prep_body(c_ref, cos_ref, sin_ref, gq_ref, gkv_ref, wq_ref, wkv_ref, vone_ref,
                   qt_ref, k_ref, vt_ref):
    c = c_ref[...]
    cq = _rms(c[:, 0:256], gq_ref[...]).astype(BF16)
    ckv = _rms(c[:, 256:384], gkv_ref[...]).astype(BF16)
    cos_t = cos_ref[...]
    sin_t = sin_ref[...]
    k_rot = c[:, 384:512] * cos_t + c[:, 512:640] * sin_t
    qq = _dot(cq, wq_ref[...])
    kvv = _dot(ckv, wkv_ref[...])
    width = MLA_HEADS * HEAD_PAD
    qs, ks = [], []
    for h in range(MLA_HEADS):
        lo, hi = h * HEAD_PAD, (h + 1) * HEAD_PAD
        qs.append((qq[:, lo:hi] * cos_t + qq[:, width + lo:width + hi] * sin_t) * MLA_Q_SCALE)
        ks.append(kvv[:, lo:hi] + k_rot)
    qt_ref[0, 0] = jnp.concatenate(qs, axis=1).T.astype(BF16)
    k_ref[...] = jnp.concatenate(ks, axis=1).astype(BF16)
    vt = (kvv[:, width:2 * width] + vone_ref[...]).T.astype(BF16)
    for d in range(vt_ref.shape[1]):
        vt_ref[0, d] = vt[:, d * MLA_TK:(d + 1) * MLA_TK]


def _mla_prep(cols, cos_t, sin_t, gq, gkv, wq, wkv, vone, bsz, seq, tm):
    tm = min(tm, seq)
    nt = seq // tm
    width = MLA_HEADS * HEAD_PAD
    row = lambda i: (i, 0)
    fixed = lambda i: (0, 0)
    return pl.pallas_call(
        _mla_prep_body,
        grid=(bsz * nt,),
        in_specs=[
            pl.BlockSpec((tm, MLA_MAIN), row),
            pl.BlockSpec((tm, LANES), row),
            pl.BlockSpec((tm, LANES), row),
            pl.BlockSpec((1, MLA_Q_LORA), fixed),
            pl.BlockSpec((1, MLA_KV_LORA), fixed),
            pl.BlockSpec((MLA_Q_LORA, 2 * width), fixed),
            pl.BlockSpec((MLA_KV_LORA, 2 * width), fixed),
            pl.BlockSpec((1, width), fixed),
        ],
        out_specs=[
            pl.BlockSpec((1, 1, width, tm), lambda i: (i // nt, i % nt, 0, 0)),
            pl.BlockSpec((tm, width), row),
            pl.BlockSpec((1, tm // MLA_TK, width, MLA_TK), lambda i: (i // nt, i % nt, 0, 0)),
        ],
        out_shape=[
            jax.ShapeDtypeStruct((bsz, nt, width, tm), BF16),
            jax.ShapeDtypeStruct((bsz * seq, width), BF16),
            jax.ShapeDtypeStruct((bsz, seq // MLA_TK, width, MLA_TK), BF16),
        ],
        compiler_params=_params(("parallel",)),
        name="mla_prep",
    )(cols, cos_t, sin_t, gq, gkv, wq, wkv, vone)


def _mla_attn_body(qt_ref, k_ref, vt_ref, o_ref, *, tq, tk):
    i = pl.program_id(1)
    sub = tq // tk
    qts = [jnp.concatenate([qt_ref[0, d, h * HEAD_PAD:(h + 1) * HEAD_PAD, :] for d in range(qt_ref.shape[1])], axis=1)
           for h in range(MLA_HEADS)]

    def step(j, carry, diag_sub):
        kblk = k_ref[0, pl.ds(pl.multiple_of(j * tk, tk), tk), :]
        if diag_sub is not None:
            visible = ((_iota((tk, tq), 0) + diag_sub * tk) // CHUNK) <= (_iota((tk, tq), 1) // CHUNK)
        out = []
        for h0 in range(0, MLA_HEADS, MLA_INTERLEAVE):
            heads = range(h0, h0 + MLA_INTERLEAVE)
            ss = {h: _dot(kblk[:, h * HEAD_PAD:(h + 1) * HEAD_PAD], qts[h]) for h in heads}
            if diag_sub is not None:
                ss = {h: jnp.where(visible, ss[h], NEG_BIG) for h in heads}
            m_new = {h: jnp.maximum(carry[h][0], jnp.max(ss[h], axis=0, keepdims=True)) for h in heads}
            ps = {h: jnp.exp2(ss[h] - m_new[h]).astype(BF16) for h in heads}
            pv = {h: _dot(vt_ref[0, j, h * HEAD_PAD:(h + 1) * HEAD_PAD, :], ps[h]) for h in heads}
            out.extend((m_new[h], jnp.exp2(carry[h][0] - m_new[h]) * carry[h][1] + pv[h]) for h in heads)
        return tuple(out)

    init = tuple((jnp.full((1, tq), NEG_BIG, F32), jnp.zeros((HEAD_PAD, tq), F32)) for _ in range(MLA_HEADS))
    carry = lax.fori_loop(0, i * sub, functools.partial(step, diag_sub=None), init)
    for d in range(sub):
        carry = step(i * sub + d, carry, d)
    outs = []
    for h in range(MLA_HEADS):
        acc = carry[h][1]
        outs.append((acc / acc[MLA_V:MLA_V + 1, :]).T)
    o_ref[...] = jnp.concatenate(outs, axis=1).astype(BF16)


def _mla_attn(qt, k, vt, bsz, seq, tq, tk):
    tq = min(tq, seq)
    nq = seq // tq
    width = MLA_HEADS * HEAD_PAD
    return pl.pallas_call(
        functools.partial(_mla_attn_body, tq=tq, tk=tk),
        grid=(bsz, nq),
        in_specs=[
            pl.BlockSpec((1, tq // qt.shape[3], width, qt.shape[3]), lambda b, i: (b, i, 0, 0)),
            pl.BlockSpec((1, seq, width), lambda b, i: (b, 0, 0)),
            pl.BlockSpec((1, seq // tk, width, tk), lambda b, i: (b, 0, 0, 0)),
        ],
        out_specs=pl.BlockSpec((tq, width), lambda b, i: (b * nq + i, 0)),
        out_shape=jax.ShapeDtypeStruct((bsz * seq, width), BF16),
        compiler_params=_params(("parallel", "arbitrary")),
        name="mla_attn",
    )(qt, k.reshape(bsz, seq, width), vt)


def _memkv_body(mem_ref, g_ref, w_ref, kt_ref, v_ref):
    mn = _rms(mem_ref[...], g_ref[...]).astype(BF16)
    kv = _dot(mn, w_ref[...])
    kt_ref[0] = kv[:, 0:D_MODEL].T.astype(BF16)
    v_ref[0] = kv[:, D_MODEL:2 * D_MODEL].astype(BF16)


def _memkv(mem2d, g, w, bsz):
    return pl.pallas_call(
        _memkv_body,
        grid=(bsz,),
        in_specs=[
            pl.BlockSpec((N_MEM, D_MODEL), lambda b: (b, 0)),
            pl.BlockSpec((1, D_MODEL), lambda b: (0, 0)),
            pl.BlockSpec((D_MODEL, 2 * D_MODEL), lambda b: (0, 0)),
        ],
        out_specs=[
            pl.BlockSpec((1, D_MODEL, N_MEM), lambda b: (b, 0, 0)),
            pl.BlockSpec((1, N_MEM, D_MODEL), lambda b: (b, 0, 0)),
        ],
        out_shape=[
            jax.ShapeDtypeStruct((bsz, D_MODEL, N_MEM), BF16),
            jax.ShapeDtypeStruct((bsz, N_MEM, D_MODEL), BF16),
        ],
        compiler_params=_params(("parallel",)),
        name="memkv",
    )(mem2d, g, w)


def _post_body(h_ref, ys_ref, ym_ref, yg_ref, wos_ref, wom_ref, wog_ref, gmix_ref, gxa_ref,
               wq_ref, kt_ref, v_ref, wo_ref, gpost_ref, o_ref):
    mix = _dot(ys_ref[...], wos_ref[...]) + _dot(ym_ref[...], wom_ref[...]) + _dot(yg_ref[...], wog_ref[...])
    h1 = h_ref[...] + _rms(mix, gmix_ref[...])
    u = _rms(h1, gxa_ref[...]).astype(BF16)
    q = (_dot(u, wq_ref[...]) * (XA_HEAD_DIM ** -0.5)).astype(BF16)
    heads = range(XA_HEADS)
    w = XA_HEAD_DIM
    ss = [_dot(q[:, hd * w:(hd + 1) * w], kt_ref[0, hd * w:(hd + 1) * w, :]) for hd in heads]
    ps = [jnp.exp(s - jnp.max(s, axis=-1, keepdims=True)) for s in ss]
    os_ = [_dot(ps[hd].astype(BF16), v_ref[0, :, hd * w:(hd + 1) * w]) for hd in heads]
    outs = [(os_[hd] / jnp.sum(ps[hd], axis=-1, keepdims=True)).astype(BF16) for hd in heads]
    xa = _dot(jnp.concatenate(outs, axis=1), wo_ref[...])
    o_ref[...] = h1 + _rms(xa, gpost_ref[...])


def _post(h, ys, ym, yg, wos, wom, wog, gmix, gxa, wq, kt, v, wo, gpost, bsz, seq, tm):
    tm = min(tm, seq)
    nt = seq // tm
    row = lambda b, t: (b * nt + t, 0)
    fixed = lambda b, t: (0, 0)
    full = lambda a: pl.BlockSpec(a.shape, fixed)
    return pl.pallas_call(
        _post_body,
        grid=(bsz, nt),
        in_specs=[
            pl.BlockSpec((tm, D_MODEL), row),
            pl.BlockSpec((tm, ys.shape[1]), row),
            pl.BlockSpec((tm, ym.shape[1]), row),
            pl.BlockSpec((tm, yg.shape[1]), row),
            full(wos), full(wom), full(wog), full(gmix), full(gxa), full(wq),
            pl.BlockSpec((1, D_MODEL, N_MEM), lambda b, t: (b, 0, 0)),
            pl.BlockSpec((1, N_MEM, D_MODEL), lambda b, t: (b, 0, 0)),
            full(wo), full(gpost),
        ],
        out_specs=pl.BlockSpec((tm, D_MODEL), row),
        out_shape=jax.ShapeDtypeStruct((bsz * seq, D_MODEL), F32),
        compiler_params=_params(("parallel", "arbitrary")),
        name="mix_out_xattn",
    )(h, ys, ym, yg, wos, wom, wog, gmix, gxa, wq, kt, v, wo, gpost)


def _pad_lanes(vals, offset, width=LANES):
    n = vals.shape[-1]
    pad = [(0, 0)] * (vals.ndim - 1) + [(offset, width - offset - n)]
    return jnp.pad(vals, pad)[..., None, :]


def _prep_params(norm_g, ffn_w_up, ffn_w_down, w_in, ssd_conv_w, ssd_conv_b, ssd_dt_bias, ssd_a_log,
                 ssd_d, ssd_norm_g, mla_q_norm_g, mla_w_uq, mla_kv_norm_g, mla_w_ukv, gdn_conv_w,
                 gdn_dt_bias, gdn_a_log, gdn_norm_g, w_out, xa_w_q, xa_w_kv, xa_w_o, tf):
    nj = D_FF // tf
    depth = norm_g.shape[0]
    vone = np.zeros((1, MLA_HEADS * HEAD_PAD), np.float32)
    vone[0, np.arange(MLA_HEADS) * HEAD_PAD + MLA_V] = 1.0
    return dict(
        norm_g=norm_g[:, :, None, :],
        w_up=ffn_w_up.astype(BF16),
        wd=ffn_w_down.astype(BF16).reshape(depth, 2, nj, tf, D_MODEL),
        w_in=_gather_cols(w_in, _IN_PERM).astype(BF16),
        ssd_cw=ssd_conv_w,
        ssd_cb=ssd_conv_b[:, None, :],
        ssd_dtb=_pad_lanes(ssd_dt_bias, SMALL_DT),
        ssd_alog=_pad_lanes(ssd_a_log, SMALL_DT),
        ssd_dsk=jnp.repeat(ssd_d, SSD_HEAD_DIM, axis=-1)[:, None, :],
        ssd_ng=ssd_norm_g[:, None, :],
        mla_gq=mla_q_norm_g[:, None, :],
        mla_gkv=mla_kv_norm_g[:, None, :],
        mla_wq=_gather_cols(mla_w_uq, _UQ_PERM).astype(BF16),
        mla_wkv=_gather_cols(mla_w_ukv, _UKV_PERM).astype(BF16),
        mla_vone=jnp.broadcast_to(jnp.asarray(vone), (depth,) + vone.shape),
        gdn_cw=gdn_conv_w,
        gdn_dtb=_pad_lanes(gdn_dt_bias, SMALL_A),
        gdn_alog=_pad_lanes(gdn_a_log, SMALL_A),
        gdn_ng=jnp.tile(gdn_norm_g, (1, GDN_HEADS))[:, None, :],
        wo_ssd=w_out[:, 0:SSD_WIDTH, :].astype(BF16),
        wo_mla=jnp.swapaxes(_gather_cols(jnp.swapaxes(w_out, 1, 2), _WOUT_MLA_ROWS), 1, 2).astype(BF16),
        wo_gdn=w_out[:, SSD_WIDTH + MLA_HEADS * MLA_V:, :].astype(BF16),
        xa_wq=xa_w_q.astype(BF16),
        xa_wkv=xa_w_kv.astype(BF16),
        xa_wo=xa_w_o.astype(BF16),
    )


def _rope_tables(positions):
    inv = 1.0 / (ROPE_THETA ** (jnp.arange(0, MLA_ROPE, 2, dtype=F32) / MLA_ROPE))
    ang = positions.astype(F32).reshape(-1, 1) * inv
    cos, sin = jnp.cos(ang), jnp.sin(ang)
    n = ang.shape[0]
    ones = jnp.ones((n, MLA_NOPE), F32)
    zeros_nope = jnp.zeros((n, MLA_NOPE), F32)
    zeros_pad = jnp.zeros((n, HEAD_PAD - MLA_NOPE - MLA_ROPE), F32)
    cos_t = jnp.concatenate([ones, cos, cos, zeros_pad], axis=1)
    sin_t = jnp.concatenate([zeros_nope, -sin, sin, zeros_pad], axis=1)
    return cos_t, sin_t


FFN_TM = 512
FFN_TF = 1408
TOKEN_TM = 512
SEQ_TILE = 256
MLA_TK = 256
MLA_TQ = 512
MLA_INTERLEAVE = 4


def kernel(x, mem, positions, norm_g, ffn_w_up, ffn_w_down, w_in, ssd_conv_w, ssd_conv_b, ssd_dt_bias,
           ssd_a_log, ssd_d, ssd_norm_g, mla_q_norm_g, mla_w_uq, mla_kv_norm_g, mla_w_ukv, gdn_conv_w,
           gdn_dt_bias, gdn_a_log, gdn_norm_g, w_out, xa_w_q, xa_w_kv, xa_w_o):
    bsz, seq, _ = x.shape
    n = bsz * seq
    cos_t, sin_t = _rope_tables(positions)
    params = _prep_params(norm_g, ffn_w_up, ffn_w_down, w_in, ssd_conv_w, ssd_conv_b, ssd_dt_bias,
                          ssd_a_log, ssd_d, ssd_norm_g, mla_q_norm_g, mla_w_uq, mla_kv_norm_g, mla_w_ukv,
                          gdn_conv_w, gdn_dt_bias, gdn_a_log, gdn_norm_g, w_out, xa_w_q, xa_w_kv, xa_w_o,
                          FFN_TF)
    mem2d = mem.reshape(bsz * N_MEM, D_MODEL)

    def layer(h, p):
        g = p["norm_g"]
        h = _ffn(h, g[FFN1_PRE], p["w_up"][0], p["wd"][0], g[FFN1_POST], FFN_TM)
        c_ssd, c_mla, c_gdn, c_small = _inproj(h, g[MIX_PRE], p["w_in"], TOKEN_TM)
        y_ssd = _ssd(c_ssd, c_small, p["ssd_cw"], p["ssd_cb"], p["ssd_dtb"], p["ssd_alog"], p["ssd_dsk"],
                     p["ssd_ng"], bsz, seq, SEQ_TILE)
        qt, kk, vt = _mla_prep(c_mla, cos_t, sin_t, p["mla_gq"], p["mla_gkv"], p["mla_wq"], p["mla_wkv"],
                               p["mla_vone"], bsz, seq, SEQ_TILE)
        y_mla = _mla_attn(qt, kk, vt, bsz, seq, MLA_TQ, MLA_TK)
        y_gdn = _gdn(c_gdn, c_small, p["gdn_cw"], p["gdn_dtb"], p["gdn_alog"], p["gdn_ng"], bsz, seq, SEQ_TILE)
        mkt, mv = _memkv(mem2d, g[MEM_NORM], p["xa_wkv"], bsz)
        h = _post(h, y_ssd, y_mla, y_gdn, p["wo_ssd"], p["wo_mla"], p["wo_gdn"], g[MIX_POST], g[XA_PRE],
                  p["xa_wq"], mkt, mv, p["xa_wo"], g[XA_POST], bsz, seq, TOKEN_TM)
        h = _ffn_resident(h, g[FFN2_PRE], p["w_up"][1], p["wd"][1].reshape(D_FF, D_MODEL), g[FFN2_POST], FFN_TM)
        return h, None

    h, _ = lax.scan(layer, x.reshape(n, D_MODEL), params)
    return h.reshape(bsz, seq, D_MODEL)
```

```python
import functools

import numpy as np
import jax
import jax.numpy as jnp
from jax import lax
from jax.experimental import pallas as pl
from jax.experimental.pallas import tpu as pltpu

F32 = jnp.float32
BF16 = jnp.bfloat16

D_MODEL = 1024
DEPTH = 4
CHUNK = 64
NORM_EPS = 1e-6
CONV_K = 4
D_FF = 2816
N_MEM = 256

SSD_HEADS = 8
SSD_HEAD_DIM = 64
SSD_WIDTH = 512
SSD_GROUPS = 2
SSD_STATE = 64
SSD_CONV_DIM = 768
SSD_IN = 1288

MLA_HEADS = 4
MLA_Q_LORA = 256
MLA_KV_LORA = 128
MLA_NOPE = 64
MLA_ROPE = 32
MLA_V = 64
MLA_IN = 416
ROPE_THETA = 10000.0
MLA_SCALE = float((MLA_NOPE + MLA_ROPE) ** -0.5)
MLA_Q_SCALE = MLA_SCALE * float(np.log2(np.e))

GDN_HEADS = 4
GDN_DK = 64
GDN_DV = 64
GDN_CONV_DIM = 768
GDN_WIDTH = 256
GDN_IN = 1032

XA_HEADS = 4
XA_HEAD_DIM = 256

(FFN1_PRE, FFN1_POST, MIX_PRE, MIX_POST, MEM_NORM, XA_PRE, XA_POST, FFN2_PRE, FFN2_POST) = range(9)

LANES = 128
HEAD_PAD = 128
NEG_BIG = -1e30
VMEM_LIMIT = 56 * 1024 * 1024

IN_COLS_PADDED = 3072
SSD_MAIN = 1280
MLA_MAIN = 640
GDN_MAIN = 1024
SMALL_DT, SMALL_B, SMALL_A = 0, 8, 12


def _in_perm():
    perm = -np.ones((IN_COLS_PADDED,), np.int64)
    perm[0:1280] = np.arange(0, 1280)
    mla0 = SSD_IN
    perm[1280:1664] = np.arange(mla0, mla0 + 384)
    kr0 = mla0 + 384
    half = MLA_ROPE // 2
    perm[1664 + 64:1664 + 96] = np.arange(kr0, kr0 + 32)
    perm[1792 + 64:1792 + 64 + half] = np.arange(kr0 + half, kr0 + 32)
    perm[1792 + 64 + half:1792 + 96] = np.arange(kr0, kr0 + half)
    gdn0 = SSD_IN + MLA_IN
    perm[1920:2944] = np.arange(gdn0, gdn0 + 1024)
    perm[2944 + SMALL_DT:2944 + SMALL_DT + 8] = np.arange(1280, 1288)
    perm[2944 + SMALL_B:2944 + SMALL_B + 4] = np.arange(gdn0 + 1024, gdn0 + 1028)
    perm[2944 + SMALL_A:2944 + SMALL_A + 4] = np.arange(gdn0 + 1028, gdn0 + 1032)
    return perm


_IN_PERM = _in_perm()


def _gather_cols(w, perm):
    valid = jnp.asarray(perm >= 0)
    idx = jnp.asarray(np.maximum(perm, 0))
    return jnp.where(valid, jnp.take(w, idx, axis=-1), 0.0)


def _uq_perms():
    a = -np.ones((MLA_HEADS * HEAD_PAD,), np.int64)
    b = -np.ones((MLA_HEADS * HEAD_PAD,), np.int64)
    hd = MLA_NOPE + MLA_ROPE
    half = MLA_ROPE // 2
    for h in range(MLA_HEADS):
        a[h * HEAD_PAD:h * HEAD_PAD + hd] = np.arange(h * hd, (h + 1) * hd)
        r0 = h * hd + MLA_NOPE
        b[h * HEAD_PAD + 64:h * HEAD_PAD + 64 + half] = np.arange(r0 + half, r0 + 32)
        b[h * HEAD_PAD + 64 + half:h * HEAD_PAD + 96] = np.arange(r0, r0 + half)
    return np.concatenate([a, b])


def _ukv_perms():
    k = -np.ones((MLA_HEADS * HEAD_PAD,), np.int64)
    v = -np.ones((MLA_HEADS * HEAD_PAD,), np.int64)
    hd = MLA_NOPE + MLA_V
    for h in range(MLA_HEADS):
        k[h * HEAD_PAD:h * HEAD_PAD + MLA_NOPE] = np.arange(h * hd, h * hd + MLA_NOPE)
        v[h * HEAD_PAD:h * HEAD_PAD + MLA_V] = np.arange(h * hd + MLA_NOPE, (h + 1) * hd)
    return np.concatenate([k, v])


_UQ_PERM = _uq_perms()
_UKV_PERM = _ukv_perms()


def _wout_mla_rows():
    rows = -np.ones((MLA_HEADS * HEAD_PAD,), np.int64)
    for h in range(MLA_HEADS):
        rows[h * HEAD_PAD:h * HEAD_PAD + MLA_V] = SSD_WIDTH + np.arange(h * MLA_V, (h + 1) * MLA_V)
    return rows


_WOUT_MLA_ROWS = _wout_mla_rows()


def _rms(x, g):
    return x * lax.rsqrt(jnp.mean(x * x, axis=-1, keepdims=True) + NORM_EPS) * g


def _silu(x):
    return x * jax.nn.sigmoid(x)


def _softplus(x):
    return jnp.maximum(x, 0.0) + jnp.log1p(jnp.exp(-jnp.abs(x)))


def _dot(a, b):
    return jnp.dot(a, b, preferred_element_type=F32)


def _split3(x):
    hi = x.astype(BF16)
    r1 = x - hi.astype(F32)
    mid = r1.astype(BF16)
    lo = (r1 - mid.astype(F32)).astype(BF16)
    return hi, mid, lo


def _dot_sel_right(x, sel):
    sel = sel.astype(BF16)
    hi, mid, lo = _split3(x)
    return _dot(hi, sel) + (_dot(mid, sel) + _dot(lo, sel))


def _dot_sel_left(sel, x):
    sel = sel.astype(BF16)
    hi, mid, lo = _split3(x)
    return _dot(sel, hi) + (_dot(sel, mid) + _dot(sel, lo))


def _iota(shape, dim):
    return lax.broadcasted_iota(jnp.int32, shape, dim)


def _params(sem):
    return pltpu.CompilerParams(dimension_semantics=sem, vmem_limit_bytes=VMEM_LIMIT)


def _ffn_body(h_ref, gpre_ref, wg_ref, wu_ref, wd_ref, gpost_ref, o_ref):
    h = h_ref[...]
    xn = _rms(h, gpre_ref[...]).astype(BF16)
    act = (_silu(_dot(xn, wg_ref[...])) * _dot(xn, wu_ref[...])).astype(BF16)
    o_ref[...] = h + 0.5 * _rms(_dot(act, wd_ref[...]), gpost_ref[...])


def _ffn(h, g_pre, w_up, w_down, g_post, tm):
    n = h.shape[0]
    tm = min(tm, n)
    resident = pl.Buffered(1)
    return pl.pallas_call(
        _ffn_body,
        grid=(n // tm,),
        in_specs=[
            pl.BlockSpec((tm, D_MODEL), lambda i: (i, 0)),
            pl.BlockSpec((1, D_MODEL), lambda i: (0, 0)),
            pl.BlockSpec((D_MODEL, D_FF), lambda i: (0, 0), pipeline_mode=resident),
            pl.BlockSpec((D_MODEL, D_FF), lambda i: (0, 1), pipeline_mode=resident),
            pl.BlockSpec((D_FF, D_MODEL), lambda i: (0, 0), pipeline_mode=resident),
            pl.BlockSpec((1, D_MODEL), lambda i: (0, 0)),
        ],
        out_specs=pl.BlockSpec((tm, D_MODEL), lambda i: (i, 0)),
        out_shape=jax.ShapeDtypeStruct((n, D_MODEL), F32),
        compiler_params=_params(("parallel",)),
        name="ffn",
    )(h, g_pre, w_up, w_up, w_down, g_post)


def _inproj_body(h_ref, g_ref, w_ref, ssd_ref, mla_ref, gdn_ref, small_ref):
    xn = _rms(h_ref[...], g_ref[...]).astype(BF16)
    y = _dot(xn, w_ref[...])
    ssd_ref[...] = y[:, 0:1280]
    mla_ref[...] = y[:, 1280:1920]
    gdn_ref[...] = y[:, 1920:2944]
    small_ref[...] = y[:, 2944:3072]


def _inproj(h, g, w, tm):
    n = h.shape[0]
    tm = min(tm, n)
    row = lambda i: (i, 0)
    fixed = lambda i: (0, 0)
    return pl.pallas_call(
        _inproj_body,
        grid=(n // tm,),
        in_specs=[
            pl.BlockSpec((tm, D_MODEL), row),
            pl.BlockSpec((1, D_MODEL), fixed),
            pl.BlockSpec((D_MODEL, IN_COLS_PADDED), fixed),
        ],
        out_specs=[
            pl.BlockSpec((tm, SSD_MAIN), row),
            pl.BlockSpec((tm, MLA_MAIN), row),
            pl.BlockSpec((tm, GDN_MAIN), row),
            pl.BlockSpec((tm, LANES), row),
        ],
        out_shape=[
            jax.ShapeDtypeStruct((n, SSD_MAIN), F32),
            jax.ShapeDtypeStruct((n, MLA_MAIN), F32),
            jax.ShapeDtypeStruct((n, GDN_MAIN), F32),
            jax.ShapeDtypeStruct((n, LANES), F32),
        ],
        compiler_params=_params(("parallel",)),
        name="inproj",
    )(h, g, w)


def _causal_conv(x, xpad_ref, cw_ref, tile):
    xpad_ref[8:8 + tile, :] = x
    acc = cw_ref[CONV_K - 1:CONV_K, :] * x
    for k in range(CONV_K - 1):
        off = 8 - (CONV_K - 1) + k
        acc = acc + cw_ref[k:k + 1, :] * xpad_ref[off:off + tile, :]
    xpad_ref[0:8, :] = x[tile - 8:tile, :]
    return acc


def _ssd_init(xpad_ref, st_ref):
    xpad_ref[0:8, :] = jnp.zeros((8, SSD_CONV_DIM), F32)
    st_ref[...] = jnp.zeros_like(st_ref)


def _ssd_body(main_ref, small_ref, cw_ref, cb_ref, dtb_ref, alog_ref, dsk_ref, ng_ref,
              y_ref, xpad_ref, st_ref, *, L):
    z = main_ref[:, 0:SSD_WIDTH]
    xbc = main_ref[:, SSD_WIDTH:SSD_MAIN]
    xa = _silu(_causal_conv(xbc, xpad_ref, cw_ref, L) + cb_ref[...])
    xs = xa[:, 0:512]
    bm = xa[:, 512:640]
    cm = xa[:, 640:768]

    dt = _softplus(small_ref[...] + dtb_ref[...])
    dta = dt * (-jnp.exp(alog_ref[...]))
    ri = _iota((L, L), 0)
    ci = _iota((L, L), 1)
    causal = ri >= ci
    acum = _dot_sel_left(causal, dta)
    expand = (_iota((LANES, SSD_WIDTH), 0) == _iota((LANES, SSD_WIDTH), 1) // SSD_HEAD_DIM).astype(F32)
    dt_x = _dot_sel_right(dt, expand)
    acum_x = _dot_sel_right(acum, expand)
    acum_t = acum.T
    bm_t = bm.T.astype(BF16)
    alast_x = acum_x[L - 1:L, :]
    xdt = xs * dt_x

    state_in = st_ref[...]
    y_off = _dot(cm.astype(BF16), state_in.astype(BF16)) * jnp.exp(acum_x)
    w_end = (xdt * jnp.exp(alast_x - acum_x)).astype(BF16)
    chunk_state = _dot(bm_t, w_end)
    own_group = (_iota((LANES, SSD_WIDTH), 0) // SSD_STATE) == (_iota((LANES, SSD_WIDTH), 1) // 256)
    st_ref[...] = jnp.where(own_group, state_in * jnp.exp(alast_x) + chunk_state, 0.0)

    lane = _iota((L, LANES), 1)
    group_of_lane = _iota((1, LANES), 1) // SSD_STATE
    heads = range(SSD_HEADS)
    hpg = SSD_HEADS // SSD_GROUPS
    cbs = [_dot(jnp.where(group_of_lane == g, cm, 0.0).astype(BF16), bm_t) for g in range(SSD_GROUPS)]
    segs = [jnp.where(causal, acum[:, hh:hh + 1] - acum_t[hh:hh + 1, :], NEG_BIG) for hh in heads]
    gs = [(cbs[hh // hpg] * jnp.exp(segs[hh])).astype(BF16) for hh in heads]
    xps = [xdt[:, slot * LANES:(slot + 1) * LANES].astype(BF16) for slot in range(SSD_HEADS // 2)]
    res = [_dot(gs[hh], xps[hh // 2]) for hh in heads]
    parts = [jnp.where(lane < SSD_HEAD_DIM, res[2 * slot], res[2 * slot + 1]) for slot in range(SSD_HEADS // 2)]
    y = jnp.concatenate(parts, axis=1) + y_off + xs * dsk_ref[...]
    y = y * _silu(z)
    outs = []
    for g in range(SSD_GROUPS):
        yg = y[:, g * 256:(g + 1) * 256]
        outs.append(yg * lax.rsqrt(jnp.mean(yg * yg, axis=-1, keepdims=True) + NORM_EPS))
    y_ref[...] = (jnp.concatenate(outs, axis=1) * ng_ref[...]).astype(BF16)


def _gdn_init(xpad_ref, s_ref):
    xpad_ref[0:8, :] = jnp.zeros((8, GDN_CONV_DIM), F32)
    s_ref[...] = jnp.zeros_like(s_ref)


def _gdn_body(main_ref, small_ref, cw_ref, dtb_ref, alog_ref, ng_ref,
              y_ref, xpad_ref, s_ref, vnew_ref, *, T):
    qkv_raw = main_ref[:, 0:GDN_CONV_DIM]
    z = main_ref[:, GDN_CONV_DIM:GDN_MAIN]
    xa = _silu(_causal_conv(qkv_raw, xpad_ref, cw_ref, T))
    q = xa[:, 0:256]
    k = xa[:, 256:512]
    v = xa[:, 512:768]

    same_head = (_iota((256, 256), 0) // GDN_DK) == (_iota((256, 256), 1) // GDN_DK)
    head_ones = same_head.astype(F32)
    qn = q * lax.rsqrt(_dot_sel_right(q * q, head_ones) + NORM_EPS) * (GDN_DK ** -0.5)
    kn = k * lax.rsqrt(_dot_sel_right(k * k, head_ones) + NORM_EPS)

    sm = small_ref[...]
    beta = jax.nn.sigmoid(sm)
    gl = -jnp.exp(alog_ref[...]) * _softplus(sm + dtb_ref[...])
    ri = _iota((T, T), 0)
    ci = _iota((T, T), 1)
    same_chunk = (ri // CHUNK) == (ci // CHUNK)
    incl = jnp.logical_and(same_chunk, ri >= ci)
    strict = jnp.logical_and(same_chunk, ri > ci)
    gcum = _dot_sel_left(incl, gl)
    er = _iota((LANES, GDN_WIDTH), 0)
    ec = _iota((LANES, GDN_WIDTH), 1) // GDN_DK
    beta_x = _dot_sel_right(beta, er == ec + SMALL_B)
    gcum_x = _dot_sel_right(gcum, er == ec + SMALL_A)
    last_sel = (ci == (ri // CHUNK) * CHUNK + (CHUNK - 1)).astype(F32)
    glast_x = _dot_sel_left(last_sel, gcum_x)
    gcum_t = gcum.T

    kb = kn * beta_x
    eg = jnp.exp(gcum_x)
    rhs = jnp.concatenate([v * beta_x, kb * eg], axis=1).astype(BF16)
    q_dec = (qn * eg).astype(BF16)
    k_dec_t = (kn * jnp.exp(glast_x - gcum_x)).T
    kn_t = kn.T.astype(BF16)
    head_of_lane = _iota((1, GDN_WIDTH), 1) // GDN_DK
    eye = (ri == ci).astype(F32)
    lower_left = []
    b = 1
    while b < CHUNK:
        in_block = (ri // (2 * b)) == (ci // (2 * b))
        lower_left.append(jnp.logical_and(in_block, (ri // b) % 2 > (ci // b) % 2))
        b *= 2

    heads = range(GDN_HEADS)
    hms = [head_of_lane == h for h in heads]
    gammas = [jnp.exp(jnp.where(incl, gcum[:, SMALL_A + h:SMALL_A + h + 1] - gcum_t[SMALL_A + h:SMALL_A + h + 1, :],
                                NEG_BIG)) for h in heads]
    kks = [_dot(jnp.where(hms[h], kb, 0.0).astype(BF16), kn_t) for h in heads]
    mlows = [jnp.where(strict, kks[h] * gammas[h], 0.0) for h in heads]
    ps = [eye - jnp.where(lower_left[0], mlows[h], 0.0) for h in heads]
    for ll in lower_left[1:]:
        pbs = [p.astype(BF16) for p in ps]
        ts = [_dot(pbs[h], jnp.where(ll, mlows[h], 0.0).astype(BF16)).astype(BF16) for h in heads]
        ps = [ps[h] - _dot(ts[h], pbs[h]) for h in heads]
    sols = [_dot(ps[h].astype(BF16), rhs) for h in heads]
    u = sols[0][:, 0:256]
    w = sols[0][:, 256:512]
    for h in range(1, GDN_HEADS):
        u = jnp.where(hms[h], sols[h][:, 0:256], u)
        w = jnp.where(hms[h], sols[h][:, 256:512], w)
    qks = [_dot(jnp.where(hms[h], qn, 0.0).astype(BF16), kn_t) for h in heads]
    qk_heads = [jnp.where(incl, qks[h] * gammas[h], 0.0).astype(BF16) for h in heads]

    vnew_ref[...] = jnp.zeros((T, GDN_WIDTH), F32)
    chunk_of_col = _iota((1, T), 1) // CHUNK
    o_inter = []
    for c in range(T // CHUNK):
        r0 = c * CHUNK
        s_in = s_ref[...]
        s_b = s_in.astype(BF16)
        vnew_ref[r0:r0 + CHUNK, :] = u[r0:r0 + CHUNK, :] - _dot(w[r0:r0 + CHUNK, :].astype(BF16), s_b)
        o_inter.append(_dot(q_dec[r0:r0 + CHUNK, :], s_b))
        kd = jnp.where(chunk_of_col == c, k_dec_t, 0.0).astype(BF16)
        upd = _dot(kd, vnew_ref[...].astype(BF16))
        dec = jnp.exp(glast_x[r0:r0 + 1, :])
        s_ref[...] = jnp.where(same_head, s_in * dec + upd, 0.0)

    vnew = vnew_ref[...].astype(BF16)
    o = jnp.concatenate(o_inter, axis=0)
    for h in range(GDN_HEADS):
        o = o + jnp.where(head_of_lane == h, _dot(qk_heads[h], vnew), 0.0)
    ms = _dot_sel_right(o * o, head_ones) * (1.0 / GDN_DV)
    y = o * lax.rsqrt(ms + NORM_EPS) * ng_ref[...] * _silu(z)
    y_ref[...] = y.astype(BF16)


def _ssd_gdn_body(ssd_main, small, ssd_cw, ssd_cb, ssd_dtb, ssd_alog, ssd_dsk, ssd_ng,
                  gdn_main, gdn_cw, gdn_dtb, gdn_alog, gdn_ng,
                  y_ssd, y_gdn, ssd_xpad, ssd_state, gdn_xpad, gdn_state, gdn_vnew, *, T):
    @pl.when(pl.program_id(1) == 0)
    def _():
        _ssd_init(ssd_xpad, ssd_state)
        _gdn_init(gdn_xpad, gdn_state)

    _ssd_body(ssd_main, small, ssd_cw, ssd_cb, ssd_dtb, ssd_alog, ssd_dsk, ssd_ng, y_ssd, ssd_xpad, ssd_state, L=T)
    _gdn_body(gdn_main, small, gdn_cw, gdn_dtb, gdn_alog, gdn_ng, y_gdn, gdn_xpad, gdn_state, gdn_vnew, T=T)


def _ssd_gdn(ssd_main, small, ssd_cw, ssd_cb, ssd_dtb, ssd_alog, ssd_dsk, ssd_ng,
             gdn_main, gdn_cw, gdn_dtb, gdn_alog, gdn_ng, bsz, seq, T):
    T = min(T, seq)
    nt = seq // T
    row = lambda b, t: (b * nt + t, 0)
    fixed = lambda b, t: (0, 0)
    return pl.pallas_call(
        functools.partial(_ssd_gdn_body, T=T),
        grid=(bsz, nt),
        in_specs=[
            pl.BlockSpec((T, SSD_MAIN), row),
            pl.BlockSpec((T, LANES), row),
            pl.BlockSpec((CONV_K, SSD_CONV_DIM), fixed),
            pl.BlockSpec((1, SSD_CONV_DIM), fixed),
            pl.BlockSpec((1, LANES), fixed),
            pl.BlockSpec((1, LANES), fixed),
            pl.BlockSpec((1, SSD_WIDTH), fixed),
            pl.BlockSpec((1, SSD_WIDTH), fixed),
            pl.BlockSpec((T, GDN_MAIN), row),
            pl.BlockSpec((CONV_K, GDN_CONV_DIM), fixed),
            pl.BlockSpec((1, LANES), fixed),
            pl.BlockSpec((1, LANES), fixed),
            pl.BlockSpec((1, GDN_WIDTH), fixed),
        ],
        out_specs=[pl.BlockSpec((T, SSD_WIDTH), row), pl.BlockSpec((T, GDN_WIDTH), row)],
        out_shape=[jax.ShapeDtypeStruct((bsz * seq, SSD_WIDTH), BF16),
                   jax.ShapeDtypeStruct((bsz * seq, GDN_WIDTH), BF16)],
        scratch_shapes=[
            pltpu.VMEM((8 + T, SSD_CONV_DIM), F32),
            pltpu.VMEM((LANES, SSD_WIDTH), F32),
            pltpu.VMEM((8 + T, GDN_CONV_DIM), F32),
            pltpu.VMEM((GDN_WIDTH, GDN_WIDTH), F32),
            pltpu.VMEM((T, GDN_WIDTH), F32),
        ],
        compiler_params=_params(("parallel", "arbitrary")),
        name="ssd_gdn",
    )(ssd_main, small, ssd_cw, ssd_cb, ssd_dtb, ssd_alog, ssd_dsk, ssd_ng, gdn_main, gdn_cw, gdn_dtb, gdn_alog, gdn_ng)


def _mla_prep_body(c_ref, cos_ref, sin_ref, gq_ref, gkv_ref, wq_ref, wkv_ref, vone_ref,
                   qt_ref, k_ref, vt_ref):
    c = c_ref[...]
    cq = _rms(c[:, 0:256], gq_ref[...]).astype(BF16)
    ckv = _rms(c[:, 256:384], gkv_ref[...]).astype(BF16)
    cos_t = cos_ref[...]
    sin_t = sin_ref[...]
    k_rot = c[:, 384:512] * cos_t + c[:, 512:640] * sin_t
    qq = _dot(cq, wq_ref[...])
    kvv = _dot(ckv, wkv_ref[...])
    width = MLA_HEADS * HEAD_PAD
    qs, ks = [], []
    for h in range(MLA_HEADS):
        lo, hi = h * HEAD_PAD, (h + 1) * HEAD_PAD
        qs.append((qq[:, lo:hi] * cos_t + qq[:, width + lo:width + hi] * sin_t) * MLA_Q_SCALE)
        ks.append(kvv[:, lo:hi] + k_rot)
    qt_ref[0, 0] = jnp.concatenate(qs, axis=1).T.astype(BF16)
    k_ref[...] = jnp.concatenate(ks, axis=1).astype(BF16)
    vt = (kvv[:, width:2 * width] + vone_ref[...]).T.astype(BF16)
    for d in range(vt_ref.shape[1]):
        vt_ref[0, d] = vt[:, d * MLA_TK:(d + 1) * MLA_TK]


def _mla_prep(cols, cos_t, sin_t, gq, gkv, wq, wkv, vone, bsz, seq, tm):
    tm = min(tm, seq)
    nt = seq // tm
    width = MLA_HEADS * HEAD_PAD
    row = lambda i: (i, 0)
    fixed = lambda i: (0, 0)
    return pl.pallas_call(
        _mla_prep_body,
        grid=(bsz * nt,),
        in_specs=[
            pl.BlockSpec((tm, MLA_MAIN), row),
            pl.BlockSpec((tm, LANES), row),
            pl.BlockSpec((tm, LANES), row),
            pl.BlockSpec((1, MLA_Q_LORA), fixed),
            pl.BlockSpec((1, MLA_KV_LORA), fixed),
            pl.BlockSpec((MLA_Q_LORA, 2 * width), fixed),
            pl.BlockSpec((MLA_KV_LORA, 2 * width), fixed),
            pl.BlockSpec((1, width), fixed),
        ],
        out_specs=[
            pl.BlockSpec((1, 1, width, tm), lambda i: (i // nt, i % nt, 0, 0)),
            pl.BlockSpec((tm, width), row),
            pl.BlockSpec((1, tm // MLA_TK, width, MLA_TK), lambda i: (i // nt, i % nt, 0, 0)),
        ],
        out_shape=[
            jax.ShapeDtypeStruct((bsz, nt, width, tm), BF16),
            jax.ShapeDtypeStruct((bsz * seq, width), BF16),
            jax.ShapeDtypeStruct((bsz, seq // MLA_TK, width, MLA_TK), BF16),
        ],
        compiler_params=_params(("parallel",)),
        name="mla_prep",
    )(cols, cos_t, sin_t, gq, gkv, wq, wkv, vone)


def _mla_attn_body(qt_ref, k_ref, vt_ref, o_ref, *, tq, tk):
    i = pl.program_id(1)
    sub = tq // tk
    qts = [jnp.concatenate([qt_ref[0, d, h * HEAD_PAD:(h + 1) * HEAD_PAD, :] for d in range(qt_ref.shape[1])], axis=1)
           for h in range(MLA_HEADS)]

    def step(j, carry, diag_sub):
        kblk = k_ref[0, pl.ds(pl.multiple_of(j * tk, tk), tk), :]
        if diag_sub is not None:
            visible = ((_iota((tk, tq), 0) + diag_sub * tk) // CHUNK) <= (_iota((tk, tq), 1) // CHUNK)
        out = []
        for h0 in range(0, MLA_HEADS, MLA_INTERLEAVE):
            heads = range(h0, h0 + MLA_INTERLEAVE)
            ss = {h: _dot(kblk[:, h * HEAD_PAD:(h + 1) * HEAD_PAD], qts[h]) for h in heads}
            if diag_sub is not None:
                ss = {h: jnp.where(visible, ss[h], NEG_BIG) for h in heads}
            m_new = {h: jnp.maximum(carry[h][0], jnp.max(ss[h], axis=0, keepdims=True)) for h in heads}
            ps = {h: jnp.exp2(ss[h] - m_new[h]).astype(BF16) for h in heads}
            pv = {h: _dot(vt_ref[0, j, h * HEAD_PAD:(h + 1) * HEAD_PAD, :], ps[h]) for h in heads}
            out.extend((m_new[h], jnp.exp2(carry[h][0] - m_new[h]) * carry[h][1] + pv[h]) for h in heads)
        return tuple(out)

    init = tuple((jnp.full((1, tq), NEG_BIG, F32), jnp.zeros((HEAD_PAD, tq), F32)) for _ in range(MLA_HEADS))
    carry = lax.fori_loop(0, i * sub, functools.partial(step, diag_sub=None), init)
    for d in range(sub):
        carry = step(i * sub + d, carry, d)
    outs = []
    for h in range(MLA_HEADS):
        acc = carry[h][1]
        outs.append((acc / acc[MLA_V:MLA_V + 1, :]).T)
    o_ref[...] = jnp.concatenate(outs, axis=1).astype(BF16)


def _mla_attn(qt, k, vt, bsz, seq, tq, tk):
    tq = min(tq, seq)
    nq = seq // tq
    width = MLA_HEADS * HEAD_PAD
    return pl.pallas_call(
        functools.partial(_mla_attn_body, tq=tq, tk=tk),
        grid=(bsz, nq),
        in_specs=[
            pl.BlockSpec((1, tq // qt.shape[3], width, qt.shape[3]), lambda b, i: (b, i, 0, 0)),
            pl.BlockSpec((1, seq, width), lambda b, i: (b, 0, 0)),
            pl.BlockSpec((1, seq // tk, width, tk), lambda b, i: (b, 0, 0, 0)),
        ],
        out_specs=pl.BlockSpec((tq, width), lambda b, i: (b * nq + i, 0)),
        out_shape=jax.ShapeDtypeStruct((bsz * seq, width), BF16),
        compiler_params=_params(("parallel", "arbitrary")),
        name="mla_attn",
    )(qt, k.reshape(bsz, seq, width), vt)


def _memkv_body(mem_ref, g_ref, w_ref, kt_ref, v_ref):
    mn = _rms(mem_ref[...], g_ref[...]).astype(BF16)
    kv = _dot(mn, w_ref[...])
    kt_ref[0] = kv[:, 0:D_MODEL].T.astype(BF16)
    v_ref[0] = kv[:, D_MODEL:2 * D_MODEL].astype(BF16)


def _memkv(mem2d, g, w, bsz):
    return pl.pallas_call(
        _memkv_body,
        grid=(bsz,),
        in_specs=[
            pl.BlockSpec((N_MEM, D_MODEL), lambda b: (b, 0)),
            pl.BlockSpec((1, D_MODEL), lambda b: (0, 0)),
            pl.BlockSpec((D_MODEL, 2 * D_MODEL), lambda b: (0, 0)),
        ],
        out_specs=[
            pl.BlockSpec((1, D_MODEL, N_MEM), lambda b: (b, 0, 0)),
            pl.BlockSpec((1, N_MEM, D_MODEL), lambda b: (b, 0, 0)),
        ],
        out_shape=[
            jax.ShapeDtypeStruct((bsz, D_MODEL, N_MEM), BF16),
            jax.ShapeDtypeStruct((bsz, N_MEM, D_MODEL), BF16),
        ],
        compiler_params=_params(("parallel",)),
        name="memkv",
    )(mem2d, g, w)


def _post_body(h_ref, ys_ref, ym_ref, yg_ref, wos_ref, wom_ref, wog_ref, gmix_ref, gxa_ref,
               wq_ref, kt_ref, v_ref, wo_ref, gpost_ref, o_ref):
    mix = _dot(ys_ref[...], wos_ref[...]) + _dot(ym_ref[...], wom_ref[...]) + _dot(yg_ref[...], wog_ref[...])
    h1 = h_ref[...] + _rms(mix, gmix_ref[...])
    u = _rms(h1, gxa_ref[...]).astype(BF16)
    q = (_dot(u, wq_ref[...]) * (XA_HEAD_DIM ** -0.5)).astype(BF16)
    heads = range(XA_HEADS)
    w = XA_HEAD_DIM
    ss = [_dot(q[:, hd * w:(hd + 1) * w], kt_ref[0, hd * w:(hd + 1) * w, :]) for hd in heads]
    ps = [jnp.exp(s - jnp.max(s, axis=-1, keepdims=True)) for s in ss]
    os_ = [_dot(ps[hd].astype(BF16), v_ref[0, :, hd * w:(hd + 1) * w]) for hd in heads]
    outs = [(os_[hd] / jnp.sum(ps[hd], axis=-1, keepdims=True)).astype(BF16) for hd in heads]
    xa = _dot(jnp.concatenate(outs, axis=1), wo_ref[...])
    o_ref[...] = h1 + _rms(xa, gpost_ref[...])


def _post(h, ys, ym, yg, wos, wom, wog, gmix, gxa, wq, kt, v, wo, gpost, bsz, seq, tm):
    tm = min(tm, seq)
    nt = seq // tm
    row = lambda b, t: (b * nt + t, 0)
    fixed = lambda b, t: (0, 0)
    full = lambda a: pl.BlockSpec(a.shape, fixed)
    return pl.pallas_call(
        _post_body,
        grid=(bsz, nt),
        in_specs=[
            pl.BlockSpec((tm, D_MODEL), row),
            pl.BlockSpec((tm, ys.shape[1]), row),
            pl.BlockSpec((tm, ym.shape[1]), row),
            pl.BlockSpec((tm, yg.shape[1]), row),
            full(wos), full(wom), full(wog), full(gmix), full(gxa), full(wq),
            pl.BlockSpec((1, D_MODEL, N_MEM), lambda b, t: (b, 0, 0)),
            pl.BlockSpec((1, N_MEM, D_MODEL), lambda b, t: (b, 0, 0)),
            full(wo), full(gpost),
        ],
        out_specs=pl.BlockSpec((tm, D_MODEL), row),
        out_shape=jax.ShapeDtypeStruct((bsz * seq, D_MODEL), F32),
        compiler_params=_params(("parallel", "arbitrary")),
        name="mix_out_xattn",
    )(h, ys, ym, yg, wos, wom, wog, gmix, gxa, wq, kt, v, wo, gpost)


def _pad_lanes(vals, offset, width=LANES):
    n = vals.shape[-1]
    pad = [(0, 0)] * (vals.ndim - 1) + [(offset, width - offset - n)]
    return jnp.pad(vals, pad)[..., None, :]


def _prep_params(norm_g, ffn_w_up, ffn_w_down, w_in, ssd_conv_w, ssd_conv_b, ssd_dt_bias, ssd_a_log,
                 ssd_d, ssd_norm_g, mla_q_norm_g, mla_w_uq, mla_kv_norm_g, mla_w_ukv, gdn_conv_w,
                 gdn_dt_bias, gdn_a_log, gdn_norm_g, w_out, xa_w_q, xa_w_kv, xa_w_o):
    depth = norm_g.shape[0]
    vone = np.zeros((1, MLA_HEADS * HEAD_PAD), np.float32)
    vone[0, np.arange(MLA_HEADS) * HEAD_PAD + MLA_V] = 1.0
    return dict(
        norm_g=norm_g[:, :, None, :],
        w_up=ffn_w_up.astype(BF16),
        wd=ffn_w_down.astype(BF16),
        w_in=_gather_cols(w_in, _IN_PERM).astype(BF16),
        ssd_cw=ssd_conv_w,
        ssd_cb=ssd_conv_b[:, None, :],
        ssd_dtb=_pad_lanes(ssd_dt_bias, SMALL_DT),
        ssd_alog=_pad_lanes(ssd_a_log, SMALL_DT),
        ssd_dsk=jnp.repeat(ssd_d, SSD_HEAD_DIM, axis=-1)[:, None, :],
        ssd_ng=ssd_norm_g[:, None, :],
        mla_gq=mla_q_norm_g[:, None, :],
        mla_gkv=mla_kv_norm_g[:, None, :],
        mla_wq=_gather_cols(mla_w_uq, _UQ_PERM).astype(BF16),
        mla_wkv=_gather_cols(mla_w_ukv, _UKV_PERM).astype(BF16),
        mla_vone=jnp.broadcast_to(jnp.asarray(vone), (depth,) + vone.shape),
        gdn_cw=gdn_conv_w,
        gdn_dtb=_pad_lanes(gdn_dt_bias, SMALL_A),
        gdn_alog=_pad_lanes(gdn_a_log, SMALL_A),
        gdn_ng=jnp.tile(gdn_norm_g, (1, GDN_HEADS))[:, None, :],
        wo_ssd=w_out[:, 0:SSD_WIDTH, :].astype(BF16),
        wo_mla=jnp.swapaxes(_gather_cols(jnp.swapaxes(w_out, 1, 2), _WOUT_MLA_ROWS), 1, 2).astype(BF16),
        wo_gdn=w_out[:, SSD_WIDTH + MLA_HEADS * MLA_V:, :].astype(BF16),
        xa_wq=xa_w_q.astype(BF16),
        xa_wkv=xa_w_kv.astype(BF16),
        xa_wo=xa_w_o.astype(BF16),
    )


def _rope_tables(positions):
    inv = 1.0 / (ROPE_THETA ** (jnp.arange(0, MLA_ROPE, 2, dtype=F32) / MLA_ROPE))
    ang = positions.astype(F32).reshape(-1, 1) * inv
    cos, sin = jnp.cos(ang), jnp.sin(ang)
    n = ang.shape[0]
    ones = jnp.ones((n, MLA_NOPE), F32)
    zeros_nope = jnp.zeros((n, MLA_NOPE), F32)
    zeros_pad = jnp.zeros((n, HEAD_PAD - MLA_NOPE - MLA_ROPE), F32)
    cos_t = jnp.concatenate([ones, cos, cos, zeros_pad], axis=1)
    sin_t = jnp.concatenate([zeros_nope, -sin, sin, zeros_pad], axis=1)
    return cos_t, sin_t


FFN_TM = 512
TOKEN_TM = 512
SEQ_TILE = 256
MLA_TK = 256
MLA_TQ = 512
MLA_INTERLEAVE = 4


def kernel(x, mem, positions, norm_g, ffn_w_up, ffn_w_down, w_in, ssd_conv_w, ssd_conv_b, ssd_dt_bias,
           ssd_a_log, ssd_d, ssd_norm_g, mla_q_norm_g, mla_w_uq, mla_kv_norm_g, mla_w_ukv, gdn_conv_w,
           gdn_dt_bias, gdn_a_log, gdn_norm_g, w_out, xa_w_q, xa_w_kv, xa_w_o):
    bsz, seq, _ = x.shape
    n = bsz * seq
    cos_t, sin_t = _rope_tables(positions)
    params = _prep_params(norm_g, ffn_w_up, ffn_w_down, w_in, ssd_conv_w, ssd_conv_b, ssd_dt_bias,
                          ssd_a_log, ssd_d, ssd_norm_g, mla_q_norm_g, mla_w_uq, mla_kv_norm_g, mla_w_ukv,
                          gdn_conv_w, gdn_dt_bias, gdn_a_log, gdn_norm_g, w_out, xa_w_q, xa_w_kv, xa_w_o)
    mem2d = mem.reshape(bsz * N_MEM, D_MODEL)

    def layer(h, p):
        g = p["norm_g"]
        h = _ffn(h, g[FFN1_PRE], p["w_up"][0], p["wd"][0], g[FFN1_POST], FFN_TM)
        c_ssd, c_mla, c_gdn, c_small = _inproj(h, g[MIX_PRE], p["w_in"], TOKEN_TM)
        y_ssd, y_gdn = _ssd_gdn(c_ssd, c_small, p["ssd_cw"], p["ssd_cb"], p["ssd_dtb"], p["ssd_alog"], p["ssd_dsk"],
                                p["ssd_ng"], c_gdn, p["gdn_cw"], p["gdn_dtb"], p["gdn_alog"], p["gdn_ng"],
                                bsz, seq, SEQ_TILE)
        qt, kk, vt = _mla_prep(c_mla, cos_t, sin_t, p["mla_gq"], p["mla_gkv"], p["mla_wq"], p["mla_wkv"],
                               p["mla_vone"], bsz, seq, SEQ_TILE)
        y_mla = _mla_attn(qt, kk, vt, bsz, seq, MLA_TQ, MLA_TK)
        mkt, mv = _memkv(mem2d, g[MEM_NORM], p["xa_wkv"], bsz)
        h = _post(h, y_ssd, y_mla, y_gdn, p["wo_ssd"], p["wo_mla"], p["wo_gdn"], g[MIX_POST], g[XA_PRE],
                  p["xa_wq"], mkt, mv, p["xa_wo"], g[XA_POST], bsz, seq, TOKEN_TM)
        h = _ffn(h, g[FFN2_PRE], p["w_up"][1], p["wd"][1], g[FFN2_POST], FFN_TM)
        return h, None

    h, _ = lax.scan(layer, x.reshape(n, D_MODEL), params)
    return h.reshape(bsz, seq, D_MODEL)
```

```python
import functools

import numpy as np
import jax
import jax.numpy as jnp
from jax import lax
from jax.experimental import pallas as pl
from jax.experimental.pallas import tpu as pltpu

F32 = jnp.float32
BF16 = jnp.bfloat16

D_MODEL = 1024
DEPTH = 4
CHUNK = 64
NORM_EPS = 1e-6
CONV_K = 4
D_FF = 2816
N_MEM = 256

SSD_HEADS = 8
SSD_HEAD_DIM = 64
SSD_WIDTH = 512
SSD_GROUPS = 2
SSD_STATE = 64
SSD_CONV_DIM = 768
SSD_IN = 1288

MLA_HEADS = 4
MLA_Q_LORA = 256
MLA_KV_LORA = 128
MLA_NOPE = 64
MLA_ROPE = 32
MLA_V = 64
MLA_IN = 416
ROPE_THETA = 10000.0
MLA_SCALE = float((MLA_NOPE + MLA_ROPE) ** -0.5)
LOG2E = float(np.log2(np.e))
MLA_Q_SCALE = MLA_SCALE * LOG2E

GDN_HEADS = 4
GDN_DK = 64
GDN_DV = 64
GDN_CONV_DIM = 768
GDN_WIDTH = 256
GDN_IN = 1032

XA_HEADS = 4
XA_HEAD_DIM = 256

(FFN1_PRE, FFN1_POST, MIX_PRE, MIX_POST, MEM_NORM, XA_PRE, XA_POST, FFN2_PRE, FFN2_POST) = range(9)

LANES = 128
HEAD_PAD = 128
NEG_BIG = -1e30
VMEM_LIMIT = 56 * 1024 * 1024

IN_COLS_PADDED = 3072
SSD_MAIN = 1280
MLA_MAIN = 640
GDN_MAIN = 1024
SMALL_DT, SMALL_B, SMALL_A = 0, 8, 12


def _in_perm():
    perm = -np.ones((IN_COLS_PADDED,), np.int64)
    perm[0:1280] = np.arange(0, 1280)
    mla0 = SSD_IN
    perm[1280:1664] = np.arange(mla0, mla0 + 384)
    kr0 = mla0 + 384
    half = MLA_ROPE // 2
    perm[1664 + 64:1664 + 96] = np.arange(kr0, kr0 + 32)
    perm[1792 + 64:1792 + 64 + half] = np.arange(kr0 + half, kr0 + 32)
    perm[1792 + 64 + half:1792 + 96] = np.arange(kr0, kr0 + half)
    gdn0 = SSD_IN + MLA_IN
    perm[1920:2944] = np.arange(gdn0, gdn0 + 1024)
    perm[2944 + SMALL_DT:2944 + SMALL_DT + 8] = np.arange(1280, 1288)
    perm[2944 + SMALL_B:2944 + SMALL_B + 4] = np.arange(gdn0 + 1024, gdn0 + 1028)
    perm[2944 + SMALL_A:2944 + SMALL_A + 4] = np.arange(gdn0 + 1028, gdn0 + 1032)
    return perm


_IN_PERM = _in_perm()


def _gather_cols(w, perm):
    valid = jnp.asarray(perm >= 0)
    idx = jnp.asarray(np.maximum(perm, 0))
    return jnp.where(valid, jnp.take(w, idx, axis=-1), 0.0)


def _uq_perms():
    a = -np.ones((MLA_HEADS * HEAD_PAD,), np.int64)
    b = -np.ones((MLA_HEADS * HEAD_PAD,), np.int64)
    hd = MLA_NOPE + MLA_ROPE
    half = MLA_ROPE // 2
    for h in range(MLA_HEADS):
        a[h * HEAD_PAD:h * HEAD_PAD + hd] = np.arange(h * hd, (h + 1) * hd)
        r0 = h * hd + MLA_NOPE
        b[h * HEAD_PAD + 64:h * HEAD_PAD + 64 + half] = np.arange(r0 + half, r0 + 32)
        b[h * HEAD_PAD + 64 + half:h * HEAD_PAD + 96] = np.arange(r0, r0 + half)
    return np.concatenate([a, b])


def _ukv_perms():
    k = -np.ones((MLA_HEADS * HEAD_PAD,), np.int64)
    v = -np.ones((MLA_HEADS * HEAD_PAD,), np.int64)
    hd = MLA_NOPE + MLA_V
    for h in range(MLA_HEADS):
        k[h * HEAD_PAD:h * HEAD_PAD + MLA_NOPE] = np.arange(h * hd, h * hd + MLA_NOPE)
        v[h * HEAD_PAD:h * HEAD_PAD + MLA_V] = np.arange(h * hd + MLA_NOPE, (h + 1) * hd)
    return np.concatenate([k, v])


_UQ_PERM = _uq_perms()
_UKV_PERM = _ukv_perms()


def _wout_mla_rows():
    rows = -np.ones((MLA_HEADS * HEAD_PAD,), np.int64)
    for h in range(MLA_HEADS):
        rows[h * HEAD_PAD:h * HEAD_PAD + MLA_V] = SSD_WIDTH + np.arange(h * MLA_V, (h + 1) * MLA_V)
    return rows


_WOUT_MLA_ROWS = _wout_mla_rows()


def _rms(x, g):
    return x * lax.rsqrt(jnp.mean(x * x, axis=-1, keepdims=True) + NORM_EPS) * g


def _silu(x):
    return x * jax.nn.sigmoid(x)


def _softplus(x):
    return jnp.maximum(x, 0.0) + jnp.log1p(jnp.exp(-jnp.abs(x)))


def _dot(a, b):
    return jnp.dot(a, b, preferred_element_type=F32)


def _split3(x):
    hi = x.astype(BF16)
    r1 = x - hi.astype(F32)
    mid = r1.astype(BF16)
    lo = (r1 - mid.astype(F32)).astype(BF16)
    return hi, mid, lo


def _dot_sel_right(x, sel):
    sel = sel.astype(BF16)
    hi, mid, lo = _split3(x)
    return _dot(hi, sel) + (_dot(mid, sel) + _dot(lo, sel))


def _dot_sel_left(sel, x):
    sel = sel.astype(BF16)
    hi, mid, lo = _split3(x)
    return _dot(sel, hi) + (_dot(sel, mid) + _dot(sel, lo))


def _iota(shape, dim):
    return lax.broadcasted_iota(jnp.int32, shape, dim)


def _params(sem):
    return pltpu.CompilerParams(dimension_semantics=sem, vmem_limit_bytes=VMEM_LIMIT)


def _ffn_body(h_ref, gpre_ref, wg_ref, wu_ref, wd_ref, gpost_ref, o_ref):
    h = h_ref[...]
    xn = _rms(h, gpre_ref[...]).astype(BF16)
    act = (_silu(_dot(xn, wg_ref[...])) * _dot(xn, wu_ref[...])).astype(BF16)
    o_ref[...] = h + 0.5 * _rms(_dot(act, wd_ref[...]), gpost_ref[...])


def _ffn(h, g_pre, w_up, w_down, g_post, tm):
    n = h.shape[0]
    tm = min(tm, n)
    resident = pl.Buffered(1)
    return pl.pallas_call(
        _ffn_body,
        grid=(n // tm,),
        in_specs=[
            pl.BlockSpec((tm, D_MODEL), lambda i: (i, 0)),
            pl.BlockSpec((1, D_MODEL), lambda i: (0, 0)),
            pl.BlockSpec((D_MODEL, D_FF), lambda i: (0, 0), pipeline_mode=resident),
            pl.BlockSpec((D_MODEL, D_FF), lambda i: (0, 1), pipeline_mode=resident),
            pl.BlockSpec((D_FF, D_MODEL), lambda i: (0, 0), pipeline_mode=resident),
            pl.BlockSpec((1, D_MODEL), lambda i: (0, 0)),
        ],
        out_specs=pl.BlockSpec((tm, D_MODEL), lambda i: (i, 0)),
        out_shape=jax.ShapeDtypeStruct((n, D_MODEL), F32),
        compiler_params=_params(("parallel",)),
        name="ffn",
    )(h, g_pre, w_up, w_up, w_down, g_post)


def _mla_project(c, cos_t, sin_t, gq, gkv, wq, wkv, vone):
    cq = _rms(c[:, 0:256], gq).astype(BF16)
    ckv = _rms(c[:, 256:384], gkv).astype(BF16)
    k_rot = c[:, 384:512] * cos_t + c[:, 512:640] * sin_t
    qq = _dot(cq, wq)
    kvv = _dot(ckv, wkv)
    width = MLA_HEADS * HEAD_PAD
    qs, ks = [], []
    for h in range(MLA_HEADS):
        lo, hi = h * HEAD_PAD, (h + 1) * HEAD_PAD
        qs.append((qq[:, lo:hi] * cos_t + qq[:, width + lo:width + hi] * sin_t) * MLA_Q_SCALE)
        ks.append(kvv[:, lo:hi] + k_rot)
    return jnp.concatenate(qs, axis=1), jnp.concatenate(ks, axis=1), kvv[:, width:2 * width] + vone


def _inproj_body(h_ref, g_ref, w_ref, cos_ref, sin_ref, gq_ref, gkv_ref, wq_ref, wkv_ref, vone_ref,
                 ssd_ref, gdn_ref, small_ref, qt_ref, k_ref, vt_ref):
    xn = _rms(h_ref[...], g_ref[...]).astype(BF16)
    y = _dot(xn, w_ref[...])
    ssd_ref[...] = y[:, 0:1280]
    gdn_ref[...] = y[:, 1920:2944]
    small_ref[...] = y[:, 2944:3072]
    q, k, v = _mla_project(y[:, 1280:1920], cos_ref[...], sin_ref[...], gq_ref[...], gkv_ref[...],
                           wq_ref[...], wkv_ref[...], vone_ref[...])
    qt_ref[0, 0] = q.T.astype(BF16)
    k_ref[...] = k.astype(BF16)
    vt = v.T.astype(BF16)
    for d in range(vt_ref.shape[1]):
        vt_ref[0, d] = vt[:, d * MLA_TK:(d + 1) * MLA_TK]


def _inproj(h, g, w, cos_t, sin_t, gq, gkv, wq, wkv, vone, bsz, seq, tm):
    tm = min(tm, seq)
    nt = seq // tm
    n = bsz * seq
    width = MLA_HEADS * HEAD_PAD
    row = lambda i: (i, 0)
    fixed = lambda i: (0, 0)
    blocked = lambda i: (i // nt, i % nt, 0, 0)
    return pl.pallas_call(
        _inproj_body,
        grid=(n // tm,),
        in_specs=[
            pl.BlockSpec((tm, D_MODEL), row),
            pl.BlockSpec((1, D_MODEL), fixed),
            pl.BlockSpec((D_MODEL, IN_COLS_PADDED), fixed),
            pl.BlockSpec((tm, LANES), row),
            pl.BlockSpec((tm, LANES), row),
            pl.BlockSpec((1, MLA_Q_LORA), fixed),
            pl.BlockSpec((1, MLA_KV_LORA), fixed),
            pl.BlockSpec((MLA_Q_LORA, 2 * width), fixed),
            pl.BlockSpec((MLA_KV_LORA, 2 * width), fixed),
            pl.BlockSpec((1, width), fixed),
        ],
        out_specs=[
            pl.BlockSpec((tm, SSD_MAIN), row),
            pl.BlockSpec((tm, GDN_MAIN), row),
            pl.BlockSpec((tm, LANES), row),
            pl.BlockSpec((1, 1, width, tm), blocked),
            pl.BlockSpec((tm, width), row),
            pl.BlockSpec((1, tm // MLA_TK, width, MLA_TK), blocked),
        ],
        out_shape=[
            jax.ShapeDtypeStruct((n, SSD_MAIN), F32),
            jax.ShapeDtypeStruct((n, GDN_MAIN), F32),
            jax.ShapeDtypeStruct((n, LANES), F32),
            jax.ShapeDtypeStruct((bsz, nt, width, tm), BF16),
            jax.ShapeDtypeStruct((n, width), BF16),
            jax.ShapeDtypeStruct((bsz, seq // MLA_TK, width, MLA_TK), BF16),
        ],
        compiler_params=_params(("parallel",)),
        name="inproj",
    )(h, g, w, cos_t, sin_t, gq, gkv, wq, wkv, vone)


def _causal_conv(x, xpad_ref, cw_ref, tile):
    xpad_ref[8:8 + tile, :] = x
    acc = cw_ref[CONV_K - 1:CONV_K, :] * x
    for k in range(CONV_K - 1):
        off = 8 - (CONV_K - 1) + k
        acc = acc + cw_ref[k:k + 1, :] * xpad_ref[off:off + tile, :]
    xpad_ref[0:8, :] = x[tile - 8:tile, :]
    return acc


def _ssd_init(xpad_ref, st_ref):
    xpad_ref[0:8, :] = jnp.zeros((8, SSD_CONV_DIM), F32)
    st_ref[...] = jnp.zeros_like(st_ref)


def _ssd_body(main_ref, small_ref, cw_ref, cb_ref, dtb_ref, alog_ref, dsk_ref, ng_ref,
              y_ref, xpad_ref, st_ref, *, L):
    z = main_ref[:, 0:SSD_WIDTH]
    xbc = main_ref[:, SSD_WIDTH:SSD_MAIN]
    xa = _silu(_causal_conv(xbc, xpad_ref, cw_ref, L) + cb_ref[...])
    xs = xa[:, 0:512]
    bm = xa[:, 512:640]
    cm = xa[:, 640:768]

    dt = _softplus(small_ref[...] + dtb_ref[...])
    dta = dt * (-jnp.exp(alog_ref[...]))
    ri = _iota((L, L), 0)
    ci = _iota((L, L), 1)
    causal = ri >= ci
    acum = _dot_sel_left(causal, dta)
    expand = (_iota((LANES, SSD_WIDTH), 0) == _iota((LANES, SSD_WIDTH), 1) // SSD_HEAD_DIM).astype(F32)
    dt_x = _dot_sel_right(dt, expand)
    acum_x = _dot_sel_right(acum, expand)
    acum2 = acum * LOG2E
    acum2_t = acum2.T
    bm_t = bm.T.astype(BF16)
    alast_x = acum_x[L - 1:L, :]
    xdt = xs * dt_x

    state_in = st_ref[...]
    y_off = _dot(cm.astype(BF16), state_in.astype(BF16)) * jnp.exp(acum_x)
    w_end = (xdt * jnp.exp(alast_x - acum_x)).astype(BF16)
    chunk_state = _dot(bm_t, w_end)
    own_group = (_iota((LANES, SSD_WIDTH), 0) // SSD_STATE) == (_iota((LANES, SSD_WIDTH), 1) // 256)
    st_ref[...] = jnp.where(own_group, state_in * jnp.exp(alast_x) + chunk_state, 0.0)

    lane = _iota((L, LANES), 1)
    group_of_lane = _iota((1, LANES), 1) // SSD_STATE
    heads = range(SSD_HEADS)
    hpg = SSD_HEADS // SSD_GROUPS
    cbs = [_dot(jnp.where(group_of_lane == g, cm, 0.0).astype(BF16), bm_t) for g in range(SSD_GROUPS)]
    segs = [jnp.where(causal, acum2[:, hh:hh + 1] - acum2_t[hh:hh + 1, :], NEG_BIG) for hh in heads]
    gs = [(cbs[hh // hpg] * jnp.exp2(segs[hh])).astype(BF16) for hh in heads]
    xps = [xdt[:, slot * LANES:(slot + 1) * LANES].astype(BF16) for slot in range(SSD_HEADS // 2)]
    res = [_dot(gs[hh], xps[hh // 2]) for hh in heads]
    parts = [jnp.where(lane < SSD_HEAD_DIM, res[2 * slot], res[2 * slot + 1]) for slot in range(SSD_HEADS // 2)]
    y = jnp.concatenate(parts, axis=1) + y_off + xs * dsk_ref[...]
    y = y * _silu(z)
    outs = []
    for g in range(SSD_GROUPS):
        yg = y[:, g * 256:(g + 1) * 256]
        outs.append(yg * lax.rsqrt(jnp.mean(yg * yg, axis=-1, keepdims=True) + NORM_EPS))
    y_ref[...] = (jnp.concatenate(outs, axis=1) * ng_ref[...]).astype(BF16)


def _gdn_init(xpad_ref, s_ref):
    xpad_ref[0:8, :] = jnp.zeros((8, GDN_CONV_DIM), F32)
    s_ref[...] = jnp.zeros_like(s_ref)


def _gdn_body(main_ref, small_ref, cw_ref, dtb_ref, alog_ref, ng_ref,
              y_ref, xpad_ref, s_ref, vnew_ref, *, T):
    qkv_raw = main_ref[:, 0:GDN_CONV_DIM]
    z = main_ref[:, GDN_CONV_DIM:GDN_MAIN]
    xa = _silu(_causal_conv(qkv_raw, xpad_ref, cw_ref, T))
    q = xa[:, 0:256]
    k = xa[:, 256:512]
    v = xa[:, 512:768]

    same_head = (_iota((256, 256), 0) // GDN_DK) == (_iota((256, 256), 1) // GDN_DK)
    head_ones = same_head.astype(F32)
    qn = q * lax.rsqrt(_dot_sel_right(q * q, head_ones) + NORM_EPS) * (GDN_DK ** -0.5)
    kn = k * lax.rsqrt(_dot_sel_right(k * k, head_ones) + NORM_EPS)

    sm = small_ref[...]
    beta = jax.nn.sigmoid(sm)
    gl = -jnp.exp(alog_ref[...]) * _softplus(sm + dtb_ref[...])
    ri = _iota((T, T), 0)
    ci = _iota((T, T), 1)
    same_chunk = (ri // CHUNK) == (ci // CHUNK)
    incl = jnp.logical_and(same_chunk, ri >= ci)
    strict = jnp.logical_and(same_chunk, ri > ci)
    gcum = _dot_sel_left(incl, gl)
    er = _iota((LANES, GDN_WIDTH), 0)
    ec = _iota((LANES, GDN_WIDTH), 1) // GDN_DK
    beta_x = _dot_sel_right(beta, er == ec + SMALL_B)
    gcum_x = _dot_sel_right(gcum, er == ec + SMALL_A)
    last_sel = (ci == (ri // CHUNK) * CHUNK + (CHUNK - 1)).astype(F32)
    glast_x = _dot_sel_left(last_sel, gcum_x)
    gcum2 = gcum * LOG2E
    gcum2_t = gcum2.T

    kb = kn * beta_x
    eg = jnp.exp(gcum_x)
    rhs = jnp.concatenate([v * beta_x, kb * eg], axis=1).astype(BF16)
    q_dec = (qn * eg).astype(BF16)
    k_dec_t = (kn * jnp.exp(glast_x - gcum_x)).T
    kn_t = kn.T.astype(BF16)
    head_of_lane = _iota((1, GDN_WIDTH), 1) // GDN_DK
    eye = (ri == ci).astype(F32)
    lower_left = []
    b = 1
    while b < CHUNK:
        in_block = (ri // (2 * b)) == (ci // (2 * b))
        lower_left.append(jnp.logical_and(in_block, (ri // b) % 2 > (ci // b) % 2))
        b *= 2

    heads = range(GDN_HEADS)
    hms = [head_of_lane == h for h in heads]
    gammas = [jnp.exp2(jnp.where(incl, gcum2[:, SMALL_A + h:SMALL_A + h + 1] - gcum2_t[SMALL_A + h:SMALL_A + h + 1, :],
                                 NEG_BIG)) for h in heads]
    kks = [_dot(jnp.where(hms[h], kb, 0.0).astype(BF16), kn_t) for h in heads]
    mlows = [jnp.where(strict, kks[h] * gammas[h], 0.0) for h in heads]
    ps = [eye - jnp.where(lower_left[0], mlows[h], 0.0) for h in heads]
    for ll in lower_left[1:]:
        pbs = [p.astype(BF16) for p in ps]
        ts = [_dot(pbs[h], jnp.where(ll, mlows[h], 0.0).astype(BF16)).astype(BF16) for h in heads]
        ps = [ps[h] - _dot(ts[h], pbs[h]) for h in heads]
    sols = [_dot(ps[h].astype(BF16), rhs) for h in heads]
    u = sols[0][:, 0:256]
    w = sols[0][:, 256:512]
    for h in range(1, GDN_HEADS):
        u = jnp.where(hms[h], sols[h][:, 0:256], u)
        w = jnp.where(hms[h], sols[h][:, 256:512], w)
    qks = [_dot(jnp.where(hms[h], qn, 0.0).astype(BF16), kn_t) for h in heads]
    qk_heads = [jnp.where(incl, qks[h] * gammas[h], 0.0).astype(BF16) for h in heads]

    vnew_ref[...] = jnp.zeros((T, GDN_WIDTH), F32)
    chunk_of_col = _iota((1, T), 1) // CHUNK
    o_inter = []
    for c in range(T // CHUNK):
        r0 = c * CHUNK
        s_in = s_ref[...]
        s_b = s_in.astype(BF16)
        vnew_ref[r0:r0 + CHUNK, :] = u[r0:r0 + CHUNK, :] - _dot(w[r0:r0 + CHUNK, :].astype(BF16), s_b)
        o_inter.append(_dot(q_dec[r0:r0 + CHUNK, :], s_b))
        kd = jnp.where(chunk_of_col == c, k_dec_t, 0.0).astype(BF16)
        upd = _dot(kd, vnew_ref[...].astype(BF16))
        dec = jnp.exp(glast_x[r0:r0 + 1, :])
        s_ref[...] = jnp.where(same_head, s_in * dec + upd, 0.0)

    vnew = vnew_ref[...].astype(BF16)
    o = jnp.concatenate(o_inter, axis=0)
    for h in range(GDN_HEADS):
        o = o + jnp.where(head_of_lane == h, _dot(qk_heads[h], vnew), 0.0)
    ms = _dot_sel_right(o * o, head_ones) * (1.0 / GDN_DV)
    y = o * lax.rsqrt(ms + NORM_EPS) * ng_ref[...] * _silu(z)
    y_ref[...] = y.astype(BF16)


def _ssd_gdn_body(ssd_main, small, ssd_cw, ssd_cb, ssd_dtb, ssd_alog, ssd_dsk, ssd_ng,
                  gdn_main, gdn_cw, gdn_dtb, gdn_alog, gdn_ng,
                  y_ssd, y_gdn, ssd_xpad, ssd_state, gdn_xpad, gdn_state, gdn_vnew, *, T):
    @pl.when(pl.program_id(1) == 0)
    def _():
        _ssd_init(ssd_xpad, ssd_state)
        _gdn_init(gdn_xpad, gdn_state)

    _ssd_body(ssd_main, small, ssd_cw, ssd_cb, ssd_dtb, ssd_alog, ssd_dsk, ssd_ng, y_ssd, ssd_xpad, ssd_state, L=T)
    _gdn_body(gdn_main, small, gdn_cw, gdn_dtb, gdn_alog, gdn_ng, y_gdn, gdn_xpad, gdn_state, gdn_vnew, T=T)


def _ssd_gdn(ssd_main, small, ssd_cw, ssd_cb, ssd_dtb, ssd_alog, ssd_dsk, ssd_ng,
             gdn_main, gdn_cw, gdn_dtb, gdn_alog, gdn_ng, bsz, seq, T):
    T = min(T, seq)
    nt = seq // T
    row = lambda b, t: (b * nt + t, 0)
    fixed = lambda b, t: (0, 0)
    return pl.pallas_call(
        functools.partial(_ssd_gdn_body, T=T),
        grid=(bsz, nt),
        in_specs=[
            pl.BlockSpec((T, SSD_MAIN), row),
            pl.BlockSpec((T, LANES), row),
            pl.BlockSpec((CONV_K, SSD_CONV_DIM), fixed),
            pl.BlockSpec((1, SSD_CONV_DIM), fixed),
            pl.BlockSpec((1, LANES), fixed),
            pl.BlockSpec((1, LANES), fixed),
            pl.BlockSpec((1, SSD_WIDTH), fixed),
            pl.BlockSpec((1, SSD_WIDTH), fixed),
            pl.BlockSpec((T, GDN_MAIN), row),
            pl.BlockSpec((CONV_K, GDN_CONV_DIM), fixed),
            pl.BlockSpec((1, LANES), fixed),
            pl.BlockSpec((1, LANES), fixed),
            pl.BlockSpec((1, GDN_WIDTH), fixed),
        ],
        out_specs=[pl.BlockSpec((T, SSD_WIDTH), row), pl.BlockSpec((T, GDN_WIDTH), row)],
        out_shape=[jax.ShapeDtypeStruct((bsz * seq, SSD_WIDTH), BF16),
                   jax.ShapeDtypeStruct((bsz * seq, GDN_WIDTH), BF16)],
        scratch_shapes=[
            pltpu.VMEM((8 + T, SSD_CONV_DIM), F32),
            pltpu.VMEM((LANES, SSD_WIDTH), F32),
            pltpu.VMEM((8 + T, GDN_CONV_DIM), F32),
            pltpu.VMEM((GDN_WIDTH, GDN_WIDTH), F32),
            pltpu.VMEM((T, GDN_WIDTH), F32),
        ],
        compiler_params=_params(("parallel", "arbitrary")),
        name="ssd_gdn",
    )(ssd_main, small, ssd_cw, ssd_cb, ssd_dtb, ssd_alog, ssd_dsk, ssd_ng, gdn_main, gdn_cw, gdn_dtb, gdn_alog, gdn_ng)


def _mla_attn_body(qt_ref, k_ref, vt_ref, o_ref, *, tq, tk):
    i = pl.program_id(1)
    sub = tq // tk
    qts = [jnp.concatenate([qt_ref[0, d, h * HEAD_PAD:(h + 1) * HEAD_PAD, :] for d in range(qt_ref.shape[1])], axis=1)
           for h in range(MLA_HEADS)]

    def step(j, carry, diag_sub):
        kblk = k_ref[0, pl.ds(pl.multiple_of(j * tk, tk), tk), :]
        if diag_sub is not None:
            visible = ((_iota((tk, tq), 0) + diag_sub * tk) // CHUNK) <= (_iota((tk, tq), 1) // CHUNK)
        out = []
        for h0 in range(0, MLA_HEADS, MLA_INTERLEAVE):
            heads = range(h0, h0 + MLA_INTERLEAVE)
            ss = {h: _dot(kblk[:, h * HEAD_PAD:(h + 1) * HEAD_PAD], qts[h]) for h in heads}
            if diag_sub is not None:
                ss = {h: jnp.where(visible, ss[h], NEG_BIG) for h in heads}
            m_new = {h: jnp.maximum(carry[h][0], jnp.max(ss[h], axis=0, keepdims=True)) for h in heads}
            ps = {h: jnp.exp2(ss[h] - m_new[h]).astype(BF16) for h in heads}
            pv = {h: _dot(vt_ref[0, j, h * HEAD_PAD:(h + 1) * HEAD_PAD, :], ps[h]) for h in heads}
            out.extend((m_new[h], jnp.exp2(carry[h][0] - m_new[h]) * carry[h][1] + pv[h]) for h in heads)
        return tuple(out)

    init = tuple((jnp.full((1, tq), NEG_BIG, F32), jnp.zeros((HEAD_PAD, tq), F32)) for _ in range(MLA_HEADS))
    carry = lax.fori_loop(0, i * sub, functools.partial(step, diag_sub=None), init)
    for d in range(sub):
        carry = step(i * sub + d, carry, d)
    outs = []
    for h in range(MLA_HEADS):
        acc = carry[h][1]
        outs.append((acc / acc[MLA_V:MLA_V + 1, :]).T)
    o_ref[...] = jnp.concatenate(outs, axis=1).astype(BF16)


def _mla_attn(qt, k, vt, bsz, seq, tq, tk):
    tq = min(tq, seq)
    nq = seq // tq
    width = MLA_HEADS * HEAD_PAD
    return pl.pallas_call(
        functools.partial(_mla_attn_body, tq=tq, tk=tk),
        grid=(bsz, nq),
        in_specs=[
            pl.BlockSpec((1, tq // qt.shape[3], width, qt.shape[3]), lambda b, i: (b, i, 0, 0)),
            pl.BlockSpec((1, seq, width), lambda b, i: (b, 0, 0)),
            pl.BlockSpec((1, seq // tk, width, tk), lambda b, i: (b, 0, 0, 0)),
        ],
        out_specs=pl.BlockSpec((tq, width), lambda b, i: (b * nq + i, 0)),
        out_shape=jax.ShapeDtypeStruct((bsz * seq, width), BF16),
        compiler_params=_params(("parallel", "arbitrary")),
        name="mla_attn",
    )(qt, k.reshape(bsz, seq, width), vt)


def _memkv_body(mem_ref, g_ref, w_ref, kt_ref, v_ref):
    mn = _rms(mem_ref[...], g_ref[...]).astype(BF16)
    kv = _dot(mn, w_ref[...])
    kt_ref[0] = kv[:, 0:D_MODEL].T.astype(BF16)
    v_ref[0] = kv[:, D_MODEL:2 * D_MODEL].astype(BF16)


def _memkv(mem2d, g, w, bsz):
    return pl.pallas_call(
        _memkv_body,
        grid=(bsz,),
        in_specs=[
            pl.BlockSpec((N_MEM, D_MODEL), lambda b: (b, 0)),
            pl.BlockSpec((1, D_MODEL), lambda b: (0, 0)),
            pl.BlockSpec((D_MODEL, 2 * D_MODEL), lambda b: (0, 0)),
        ],
        out_specs=[
            pl.BlockSpec((1, D_MODEL, N_MEM), lambda b: (b, 0, 0)),
            pl.BlockSpec((1, N_MEM, D_MODEL), lambda b: (b, 0, 0)),
        ],
        out_shape=[
            jax.ShapeDtypeStruct((bsz, D_MODEL, N_MEM), BF16),
            jax.ShapeDtypeStruct((bsz, N_MEM, D_MODEL), BF16),
        ],
        compiler_params=_params(("parallel",)),
        name="memkv",
    )(mem2d, g, w)


def _post_body(h_ref, ys_ref, ym_ref, yg_ref, wos_ref, wom_ref, wog_ref, gmix_ref, gxa_ref,
               wq_ref, kt_ref, v_ref, wo_ref, gpost_ref, o_ref):
    mix = _dot(ys_ref[...], wos_ref[...]) + _dot(ym_ref[...], wom_ref[...]) + _dot(yg_ref[...], wog_ref[...])
    h1 = h_ref[...] + _rms(mix, gmix_ref[...])
    u = _rms(h1, gxa_ref[...]).astype(BF16)
    q = (_dot(u, wq_ref[...]) * (XA_HEAD_DIM ** -0.5)).astype(BF16)
    heads = range(XA_HEADS)
    w = XA_HEAD_DIM
    ss = [_dot(q[:, hd * w:(hd + 1) * w], kt_ref[0, hd * w:(hd + 1) * w, :]) for hd in heads]
    ps = [jnp.exp(s - jnp.max(s, axis=-1, keepdims=True)) for s in ss]
    os_ = [_dot(ps[hd].astype(BF16), v_ref[0, :, hd * w:(hd + 1) * w]) for hd in heads]
    outs = [(os_[hd] / jnp.sum(ps[hd], axis=-1, keepdims=True)).astype(BF16) for hd in heads]
    xa = _dot(jnp.concatenate(outs, axis=1), wo_ref[...])
    o_ref[...] = h1 + _rms(xa, gpost_ref[...])


def _post(h, ys, ym, yg, wos, wom, wog, gmix, gxa, wq, kt, v, wo, gpost, bsz, seq, tm):
    tm = min(tm, seq)
    nt = seq // tm
    row = lambda b, t: (b * nt + t, 0)
    fixed = lambda b, t: (0, 0)
    full = lambda a: pl.BlockSpec(a.shape, fixed)
    return pl.pallas_call(
        _post_body,
        grid=(bsz, nt),
        in_specs=[
            pl.BlockSpec((tm, D_MODEL), row),
            pl.BlockSpec((tm, ys.shape[1]), row),
            pl.BlockSpec((tm, ym.shape[1]), row),
            pl.BlockSpec((tm, yg.shape[1]), row),
            full(wos), full(wom), full(wog), full(gmix), full(gxa), full(wq),
            pl.BlockSpec((1, D_MODEL, N_MEM), lambda b, t: (b, 0, 0)),
            pl.BlockSpec((1, N_MEM, D_MODEL), lambda b, t: (b, 0, 0)),
            full(wo), full(gpost),
        ],
        out_specs=pl.BlockSpec((tm, D_MODEL), row),
        out_shape=jax.ShapeDtypeStruct((bsz * seq, D_MODEL), F32),
        compiler_params=_params(("parallel", "arbitrary")),
        name="mix_out_xattn",
    )(h, ys, ym, yg, wos, wom, wog, gmix, gxa, wq, kt, v, wo, gpost)


def _pad_lanes(vals, offset, width=LANES):
    n = vals.shape[-1]
    pad = [(0, 0)] * (vals.ndim - 1) + [(offset, width - offset - n)]
    return jnp.pad(vals, pad)[..., None, :]


def _prep_params(norm_g, ffn_w_up, ffn_w_down, w_in, ssd_conv_w, ssd_conv_b, ssd_dt_bias, ssd_a_log,
                 ssd_d, ssd_norm_g, mla_q_norm_g, mla_w_uq, mla_kv_norm_g, mla_w_ukv, gdn_conv_w,
                 gdn_dt_bias, gdn_a_log, gdn_norm_g, w_out, xa_w_q, xa_w_kv, xa_w_o):
    depth = norm_g.shape[0]
    vone = np.zeros((1, MLA_HEADS * HEAD_PAD), np.float32)
    vone[0, np.arange(MLA_HEADS) * HEAD_PAD + MLA_V] = 1.0
    return dict(
        norm_g=norm_g[:, :, None, :],
        w_up=ffn_w_up.astype(BF16),
        wd=ffn_w_down.astype(BF16),
        w_in=_gather_cols(w_in, _IN_PERM).astype(BF16),
        ssd_cw=ssd_conv_w,
        ssd_cb=ssd_conv_b[:, None, :],
        ssd_dtb=_pad_lanes(ssd_dt_bias, SMALL_DT),
        ssd_alog=_pad_lanes(ssd_a_log, SMALL_DT),
        ssd_dsk=jnp.repeat(ssd_d, SSD_HEAD_DIM, axis=-1)[:, None, :],
        ssd_ng=ssd_norm_g[:, None, :],
        mla_gq=mla_q_norm_g[:, None, :],
        mla_gkv=mla_kv_norm_g[:, None, :],
        mla_wq=_gather_cols(mla_w_uq, _UQ_PERM).astype(BF16),
        mla_wkv=_gather_cols(mla_w_ukv, _UKV_PERM).astype(BF16),
        mla_vone=jnp.broadcast_to(jnp.asarray(vone), (depth,) + vone.shape),
        gdn_cw=gdn_conv_w,
        gdn_dtb=_pad_lanes(gdn_dt_bias, SMALL_A),
        gdn_alog=_pad_lanes(gdn_a_log, SMALL_A),
        gdn_ng=jnp.tile(gdn_norm_g, (1, GDN_HEADS))[:, None, :],
        wo_ssd=w_out[:, 0:SSD_WIDTH, :].astype(BF16),
        wo_mla=jnp.swapaxes(_gather_cols(jnp.swapaxes(w_out, 1, 2), _WOUT_MLA_ROWS), 1, 2).astype(BF16),
        wo_gdn=w_out[:, SSD_WIDTH + MLA_HEADS * MLA_V:, :].astype(BF16),
        xa_wq=xa_w_q.astype(BF16),
        xa_wkv=xa_w_kv.astype(BF16),
        xa_wo=xa_w_o.astype(BF16),
    )


def _rope_tables(positions):
    inv = 1.0 / (ROPE_THETA ** (jnp.arange(0, MLA_ROPE, 2, dtype=F32) / MLA_ROPE))
    ang = positions.astype(F32).reshape(-1, 1) * inv
    cos, sin = jnp.cos(ang), jnp.sin(ang)
    n = ang.shape[0]
    ones = jnp.ones((n, MLA_NOPE), F32)
    zeros_nope = jnp.zeros((n, MLA_NOPE), F32)
    zeros_pad = jnp.zeros((n, HEAD_PAD - MLA_NOPE - MLA_ROPE), F32)
    cos_t = jnp.concatenate([ones, cos, cos, zeros_pad], axis=1)
    sin_t = jnp.concatenate([zeros_nope, -sin, sin, zeros_pad], axis=1)
    return cos_t, sin_t


FFN_TM = 512
TOKEN_TM = 512
SEQ_TILE = 256
MLA_TK = 256
MLA_TQ = 512
MLA_INTERLEAVE = 4


def kernel(x, mem, positions, norm_g, ffn_w_up, ffn_w_down, w_in, ssd_conv_w, ssd_conv_b, ssd_dt_bias,
           ssd_a_log, ssd_d, ssd_norm_g, mla_q_norm_g, mla_w_uq, mla_kv_norm_g, mla_w_ukv, gdn_conv_w,
           gdn_dt_bias, gdn_a_log, gdn_norm_g, w_out, xa_w_q, xa_w_kv, xa_w_o):
    bsz, seq, _ = x.shape
    n = bsz * seq
    cos_t, sin_t = _rope_tables(positions)
    params = _prep_params(norm_g, ffn_w_up, ffn_w_down, w_in, ssd_conv_w, ssd_conv_b, ssd_dt_bias,
                          ssd_a_log, ssd_d, ssd_norm_g, mla_q_norm_g, mla_w_uq, mla_kv_norm_g, mla_w_ukv,
                          gdn_conv_w, gdn_dt_bias, gdn_a_log, gdn_norm_g, w_out, xa_w_q, xa_w_kv, xa_w_o)
    mem2d = mem.reshape(bsz * N_MEM, D_MODEL)

    def layer(h, p):
        g = p["norm_g"]
        h = _ffn(h, g[FFN1_PRE], p["w_up"][0], p["wd"][0], g[FFN1_POST], FFN_TM)
        c_ssd, c_gdn, c_small, qt, kk, vt = _inproj(h, g[MIX_PRE], p["w_in"], cos_t, sin_t, p["mla_gq"], p["mla_gkv"],
                                                    p["mla_wq"], p["mla_wkv"], p["mla_vone"], bsz, seq, TOKEN_TM)
        y_ssd, y_gdn = _ssd_gdn(c_ssd, c_small, p["ssd_cw"], p["ssd_cb"], p["ssd_dtb"], p["ssd_alog"], p["ssd_dsk"],
                                p["ssd_ng"], c_gdn, p["gdn_cw"], p["gdn_dtb"], p["gdn_alog"], p["gdn_ng"],
                                bsz, seq, SEQ_TILE)
        y_mla = _mla_attn(qt, kk, vt, bsz, seq, MLA_TQ, MLA_TK)
        mkt, mv = _memkv(mem2d, g[MEM_NORM], p["xa_wkv"], bsz)
        h = _post(h, y_ssd, y_mla, y_gdn, p["wo_ssd"], p["wo_mla"], p["wo_gdn"], g[MIX_POST], g[XA_PRE],
                  p["xa_wq"], mkt, mv, p["xa_wo"], g[XA_POST], bsz, seq, TOKEN_TM)
        h = _ffn(h, g[FFN2_PRE], p["w_up"][1], p["wd"][1], g[FFN2_POST], FFN_TM)
        return h, None

    h, _ = lax.scan(layer, x.reshape(n, D_MODEL), params)
    return h.reshape(bsz, seq, D_MODEL)
```

```python
import functools

import numpy as np
import jax
import jax.numpy as jnp
from jax import lax
from jax.experimental import pallas as pl
from jax.experimental.pallas import tpu as pltpu

F32 = jnp.float32
BF16 = jnp.bfloat16

D_MODEL = 1024
DEPTH = 4
CHUNK = 64
NORM_EPS = 1e-6
CONV_K = 4
D_FF = 2816
N_MEM = 256

SSD_HEADS = 8
SSD_HEAD_DIM = 64
SSD_WIDTH = 512
SSD_GROUPS = 2
SSD_STATE = 64
SSD_CONV_DIM = 768
SSD_IN = 1288

MLA_HEADS = 4
MLA_Q_LORA = 256
MLA_KV_LORA = 128
MLA_NOPE = 64
MLA_ROPE = 32
MLA_V = 64
MLA_IN = 416
ROPE_THETA = 10000.0
MLA_SCALE = float((MLA_NOPE + MLA_ROPE) ** -0.5)
LOG2E = float(np.log2(np.e))
MLA_Q_SCALE = MLA_SCALE * LOG2E

GDN_HEADS = 4
GDN_DK = 64
GDN_DV = 64
GDN_CONV_DIM = 768
GDN_WIDTH = 256
GDN_IN = 1032

XA_HEADS = 4
XA_HEAD_DIM = 256

(FFN1_PRE, FFN1_POST, MIX_PRE, MIX_POST, MEM_NORM, XA_PRE, XA_POST, FFN2_PRE, FFN2_POST) = range(9)

LANES = 128
HEAD_PAD = 128
V_ROWS = 80
NEG_BIG = -1e30
VMEM_LIMIT = 56 * 1024 * 1024

IN_COLS_PADDED = 3072
SSD_MAIN = 1280
GDN_MAIN = 1024
SMALL_DT, SMALL_B, SMALL_A = 0, 8, 12


def _in_perm():
    perm = -np.ones((IN_COLS_PADDED,), np.int64)
    perm[0:1280] = np.arange(0, 1280)
    mla0 = SSD_IN
    perm[1280:1664] = np.arange(mla0, mla0 + 384)
    kr0 = mla0 + 384
    half = MLA_ROPE // 2
    perm[1664 + 64:1664 + 96] = np.arange(kr0, kr0 + 32)
    perm[1792 + 64:1792 + 64 + half] = np.arange(kr0 + half, kr0 + 32)
    perm[1792 + 64 + half:1792 + 96] = np.arange(kr0, kr0 + half)
    gdn0 = SSD_IN + MLA_IN
    perm[1920:2944] = np.arange(gdn0, gdn0 + 1024)
    perm[2944 + SMALL_DT:2944 + SMALL_DT + 8] = np.arange(1280, 1288)
    perm[2944 + SMALL_B:2944 + SMALL_B + 4] = np.arange(gdn0 + 1024, gdn0 + 1028)
    perm[2944 + SMALL_A:2944 + SMALL_A + 4] = np.arange(gdn0 + 1028, gdn0 + 1032)
    return perm


_IN_PERM = _in_perm()


def _gather_cols(w, perm):
    valid = jnp.asarray(perm >= 0)
    idx = jnp.asarray(np.maximum(perm, 0))
    return jnp.where(valid, jnp.take(w, idx, axis=-1), 0.0)


def _uq_perms():
    a = -np.ones((MLA_HEADS * HEAD_PAD,), np.int64)
    b = -np.ones((MLA_HEADS * HEAD_PAD,), np.int64)
    hd = MLA_NOPE + MLA_ROPE
    half = MLA_ROPE // 2
    for h in range(MLA_HEADS):
        a[h * HEAD_PAD:h * HEAD_PAD + hd] = np.arange(h * hd, (h + 1) * hd)
        r0 = h * hd + MLA_NOPE
        b[h * HEAD_PAD + 64:h * HEAD_PAD + 64 + half] = np.arange(r0 + half, r0 + 32)
        b[h * HEAD_PAD + 64 + half:h * HEAD_PAD + 96] = np.arange(r0, r0 + half)
    return np.concatenate([a, b])


def _ukv_perms():
    k = -np.ones((MLA_HEADS * HEAD_PAD,), np.int64)
    v = -np.ones((MLA_HEADS * HEAD_PAD,), np.int64)
    hd = MLA_NOPE + MLA_V
    for h in range(MLA_HEADS):
        k[h * HEAD_PAD:h * HEAD_PAD + MLA_NOPE] = np.arange(h * hd, h * hd + MLA_NOPE)
        v[h * HEAD_PAD:h * HEAD_PAD + MLA_V] = np.arange(h * hd + MLA_NOPE, (h + 1) * hd)
    return np.concatenate([k, v])


_UQ_PERM = _uq_perms()
_UKV_PERM = _ukv_perms()


def _wout_mla_rows():
    rows = -np.ones((MLA_HEADS * HEAD_PAD,), np.int64)
    for h in range(MLA_HEADS):
        rows[h * HEAD_PAD:h * HEAD_PAD + MLA_V] = SSD_WIDTH + np.arange(h * MLA_V, (h + 1) * MLA_V)
    return rows


_WOUT_MLA_ROWS = _wout_mla_rows()


def _rms(x, g):
    return x * lax.rsqrt(jnp.mean(x * x, axis=-1, keepdims=True) + NORM_EPS) * g


def _silu(x):
    return x * jax.nn.sigmoid(x)


def _softplus(x):
    return jnp.maximum(x, 0.0) + jnp.log1p(jnp.exp(-jnp.abs(x)))


def _dot(a, b):
    return jnp.dot(a, b, preferred_element_type=F32)


def _split3(x):
    hi = x.astype(BF16)
    r1 = x - hi.astype(F32)
    mid = r1.astype(BF16)
    lo = (r1 - mid.astype(F32)).astype(BF16)
    return hi, mid, lo


def _dot_sel_right(x, sel):
    sel = sel.astype(BF16)
    hi, mid, lo = _split3(x)
    return _dot(hi, sel) + (_dot(mid, sel) + _dot(lo, sel))


def _dot_sel_left(sel, x):
    sel = sel.astype(BF16)
    hi, mid, lo = _split3(x)
    return _dot(sel, hi) + (_dot(sel, mid) + _dot(sel, lo))


def _iota(shape, dim):
    return lax.broadcasted_iota(jnp.int32, shape, dim)


def _params(sem):
    return pltpu.CompilerParams(dimension_semantics=sem, vmem_limit_bytes=VMEM_LIMIT)


def _ffn_body(h_ref, gpre_ref, wg_ref, wu_ref, wd_ref, gpost_ref, o_ref):
    h = h_ref[...]
    xn = _rms(h, gpre_ref[...]).astype(BF16)
    act = (_silu(_dot(xn, wg_ref[...])) * _dot(xn, wu_ref[...])).astype(BF16)
    o_ref[...] = h + 0.5 * _rms(_dot(act, wd_ref[...]), gpost_ref[...])


def _ffn(h, g_pre, w_up, w_down, g_post, tm):
    n = h.shape[0]
    tm = min(tm, n)
    resident = pl.Buffered(1)
    return pl.pallas_call(
        _ffn_body,
        grid=(n // tm,),
        in_specs=[
            pl.BlockSpec((tm, D_MODEL), lambda i: (i, 0)),
            pl.BlockSpec((1, D_MODEL), lambda i: (0, 0)),
            pl.BlockSpec((D_MODEL, D_FF), lambda i: (0, 0), pipeline_mode=resident),
            pl.BlockSpec((D_MODEL, D_FF), lambda i: (0, 1), pipeline_mode=resident),
            pl.BlockSpec((D_FF, D_MODEL), lambda i: (0, 0), pipeline_mode=resident),
            pl.BlockSpec((1, D_MODEL), lambda i: (0, 0)),
        ],
        out_specs=pl.BlockSpec((tm, D_MODEL), lambda i: (i, 0)),
        out_shape=jax.ShapeDtypeStruct((n, D_MODEL), F32),
        compiler_params=_params(("parallel",)),
        name="ffn",
    )(h, g_pre, w_up, w_up, w_down, g_post)


def _mla_project(c, cos_t, sin_t, gq, gkv, wq, wkv, vone):
    cq = _rms(c[:, 0:256], gq).astype(BF16)
    ckv = _rms(c[:, 256:384], gkv).astype(BF16)
    k_rot = c[:, 384:512] * cos_t + c[:, 512:640] * sin_t
    qq = _dot(cq, wq)
    kvv = _dot(ckv, wkv)
    width = MLA_HEADS * HEAD_PAD
    qs, ks = [], []
    for h in range(MLA_HEADS):
        lo, hi = h * HEAD_PAD, (h + 1) * HEAD_PAD
        qs.append((qq[:, lo:hi] * cos_t + qq[:, width + lo:width + hi] * sin_t) * MLA_Q_SCALE)
        ks.append(kvv[:, lo:hi] + k_rot)
    return jnp.concatenate(qs, axis=1), jnp.concatenate(ks, axis=1), kvv[:, width:2 * width] + vone


def _inproj_body(h_ref, g_ref, w_ref, cos_ref, sin_ref, gq_ref, gkv_ref, wq_ref, wkv_ref, vone_ref,
                 ssd_ref, gdn_ref, small_ref, qt_ref, k_ref, vt_ref):
    xn = _rms(h_ref[...], g_ref[...]).astype(BF16)
    y = _dot(xn, w_ref[...])
    ssd_ref[...] = y[:, 0:1280]
    gdn_ref[...] = y[:, 1920:2944]
    small_ref[...] = y[:, 2944:3072]
    q, k, v = _mla_project(y[:, 1280:1920], cos_ref[...], sin_ref[...], gq_ref[...], gkv_ref[...],
                           wq_ref[...], wkv_ref[...], vone_ref[...])
    qt_ref[0, 0] = q.T.astype(BF16)
    k_ref[...] = k.astype(BF16)
    vt_full = v.T.astype(BF16)
    vt = jnp.concatenate([vt_full[h * HEAD_PAD:h * HEAD_PAD + V_ROWS, :] for h in range(MLA_HEADS)], axis=0)
    for d in range(vt_ref.shape[1]):
        vt_ref[0, d] = vt[:, d * MLA_TK:(d + 1) * MLA_TK]


def _inproj(h, g, w, cos_t, sin_t, gq, gkv, wq, wkv, vone, bsz, seq, tm):
    tm = min(tm, seq)
    nt = seq // tm
    n = bsz * seq
    width = MLA_HEADS * HEAD_PAD
    row = lambda i: (i, 0)
    fixed = lambda i: (0, 0)
    blocked = lambda i: (i // nt, i % nt, 0, 0)
    return pl.pallas_call(
        _inproj_body,
        grid=(n // tm,),
        in_specs=[
            pl.BlockSpec((tm, D_MODEL), row),
            pl.BlockSpec((1, D_MODEL), fixed),
            pl.BlockSpec((D_MODEL, IN_COLS_PADDED), fixed),
            pl.BlockSpec((tm, LANES), row),
            pl.BlockSpec((tm, LANES), row),
            pl.BlockSpec((1, MLA_Q_LORA), fixed),
            pl.BlockSpec((1, MLA_KV_LORA), fixed),
            pl.BlockSpec((MLA_Q_LORA, 2 * width), fixed),
            pl.BlockSpec((MLA_KV_LORA, 2 * width), fixed),
            pl.BlockSpec((1, width), fixed),
        ],
        out_specs=[
            pl.BlockSpec((tm, SSD_MAIN), row),
            pl.BlockSpec((tm, GDN_MAIN), row),
            pl.BlockSpec((tm, LANES), row),
            pl.BlockSpec((1, 1, width, tm), blocked),
            pl.BlockSpec((tm, width), row),
            pl.BlockSpec((1, tm // MLA_TK, MLA_HEADS * V_ROWS, MLA_TK), blocked),
        ],
        out_shape=[
            jax.ShapeDtypeStruct((n, SSD_MAIN), F32),
            jax.ShapeDtypeStruct((n, GDN_MAIN), F32),
            jax.ShapeDtypeStruct((n, LANES), F32),
            jax.ShapeDtypeStruct((bsz, nt, width, tm), BF16),
            jax.ShapeDtypeStruct((n, width), BF16),
            jax.ShapeDtypeStruct((bsz, seq // MLA_TK, MLA_HEADS * V_ROWS, MLA_TK), BF16),
        ],
        compiler_params=_params(("parallel",)),
        name="inproj",
    )(h, g, w, cos_t, sin_t, gq, gkv, wq, wkv, vone)


def _causal_conv(x, xpad_ref, cw_ref, tile):
    xpad_ref[8:8 + tile, :] = x
    acc = cw_ref[CONV_K - 1:CONV_K, :] * x
    for k in range(CONV_K - 1):
        off = 8 - (CONV_K - 1) + k
        acc = acc + cw_ref[k:k + 1, :] * xpad_ref[off:off + tile, :]
    xpad_ref[0:8, :] = x[tile - 8:tile, :]
    return acc


def _ssd_init(xpad_ref, st_ref):
    xpad_ref[0:8, :] = jnp.zeros((8, SSD_CONV_DIM), F32)
    st_ref[...] = jnp.zeros_like(st_ref)


def _ssd_body(main_ref, small_ref, cw_ref, cb_ref, dtb_ref, alog_ref, dsk_ref, ng_ref,
              y_ref, xpad_ref, st_ref, *, L):
    z = main_ref[:, 0:SSD_WIDTH]
    xbc = main_ref[:, SSD_WIDTH:SSD_MAIN]
    xa = _silu(_causal_conv(xbc, xpad_ref, cw_ref, L) + cb_ref[...])
    xs = xa[:, 0:512]
    bm = xa[:, 512:640]
    cm = xa[:, 640:768]

    dt = _softplus(small_ref[...] + dtb_ref[...])
    dta = dt * (-jnp.exp(alog_ref[...]))
    ri = _iota((L, L), 0)
    ci = _iota((L, L), 1)
    causal = ri >= ci
    acum = _dot_sel_left(causal, dta)
    expand = (_iota((LANES, SSD_WIDTH), 0) == _iota((LANES, SSD_WIDTH), 1) // SSD_HEAD_DIM).astype(F32)
    dt_x = _dot_sel_right(dt, expand)
    acum_x = _dot_sel_right(acum, expand)
    acum2 = acum * LOG2E
    acum2_t = acum2.T
    bm_t = bm.T.astype(BF16)
    alast_x = acum_x[L - 1:L, :]
    xdt = xs * dt_x

    state_in = st_ref[...]
    y_off = _dot(cm.astype(BF16), state_in.astype(BF16)) * jnp.exp(acum_x)
    w_end = (xdt * jnp.exp(alast_x - acum_x)).astype(BF16)
    chunk_state = _dot(bm_t, w_end)
    own_group = (_iota((LANES, SSD_WIDTH), 0) // SSD_STATE) == (_iota((LANES, SSD_WIDTH), 1) // 256)
    st_ref[...] = jnp.where(own_group, state_in * jnp.exp(alast_x) + chunk_state, 0.0)

    lane = _iota((L, LANES), 1)
    group_of_lane = _iota((1, LANES), 1) // SSD_STATE
    heads = range(SSD_HEADS)
    hpg = SSD_HEADS // SSD_GROUPS
    cbs = [_dot(jnp.where(group_of_lane == g, cm, 0.0).astype(BF16), bm_t) for g in range(SSD_GROUPS)]
    segs = [jnp.where(causal, acum2[:, hh:hh + 1] - acum2_t[hh:hh + 1, :], NEG_BIG) for hh in heads]
    gs = [(cbs[hh // hpg] * jnp.exp2(segs[hh])).astype(BF16) for hh in heads]
    xps = [xdt[:, slot * LANES:(slot + 1) * LANES].astype(BF16) for slot in range(SSD_HEADS // 2)]
    res = [_dot(gs[hh], xps[hh // 2]) for hh in heads]
    parts = [jnp.where(lane < SSD_HEAD_DIM, res[2 * slot], res[2 * slot + 1]) for slot in range(SSD_HEADS // 2)]
    y = jnp.concatenate(parts, axis=1) + y_off + xs * dsk_ref[...]
    y = y * _silu(z)
    outs = []
    for g in range(SSD_GROUPS):
        yg = y[:, g * 256:(g + 1) * 256]
        outs.append(yg * lax.rsqrt(jnp.mean(yg * yg, axis=-1, keepdims=True) + NORM_EPS))
    y_ref[...] = (jnp.concatenate(outs, axis=1) * ng_ref[...]).astype(BF16)


def _gdn_init(xpad_ref, s_ref):
    xpad_ref[0:8, :] = jnp.zeros((8, GDN_CONV_DIM), F32)
    s_ref[...] = jnp.zeros_like(s_ref)


def _gdn_body(main_ref, small_ref, cw_ref, dtb_ref, alog_ref, ng_ref,
              y_ref, xpad_ref, s_ref, vnew_ref, *, T):
    qkv_raw = main_ref[:, 0:GDN_CONV_DIM]
    z = main_ref[:, GDN_CONV_DIM:GDN_MAIN]
    xa = _silu(_causal_conv(qkv_raw, xpad_ref, cw_ref, T))
    q = xa[:, 0:256]
    k = xa[:, 256:512]
    v = xa[:, 512:768]

    same_head = (_iota((256, 256), 0) // GDN_DK) == (_iota((256, 256), 1) // GDN_DK)
    head_ones = same_head.astype(F32)
    qn = q * lax.rsqrt(_dot_sel_right(q * q, head_ones) + NORM_EPS) * (GDN_DK ** -0.5)
    kn = k * lax.rsqrt(_dot_sel_right(k * k, head_ones) + NORM_EPS)

    sm = small_ref[...]
    beta = jax.nn.sigmoid(sm)
    gl = -jnp.exp(alog_ref[...]) * _softplus(sm + dtb_ref[...])
    ri = _iota((T, T), 0)
    ci = _iota((T, T), 1)
    same_chunk = (ri // CHUNK) == (ci // CHUNK)
    incl = jnp.logical_and(same_chunk, ri >= ci)
    strict = jnp.logical_and(same_chunk, ri > ci)
    gcum = _dot_sel_left(incl, gl)
    er = _iota((LANES, GDN_WIDTH), 0)
    ec = _iota((LANES, GDN_WIDTH), 1) // GDN_DK
    beta_x = _dot_sel_right(beta, er == ec + SMALL_B)
    gcum_x = _dot_sel_right(gcum, er == ec + SMALL_A)
    last_sel = (ci == (ri // CHUNK) * CHUNK + (CHUNK - 1)).astype(F32)
    glast_x = _dot_sel_left(last_sel, gcum_x)
    gcum2 = gcum * LOG2E
    gcum2_t = gcum2.T

    kb = kn * beta_x
    eg = jnp.exp(gcum_x)
    rhs = jnp.concatenate([v * beta_x, kb * eg], axis=1).astype(BF16)
    q_dec = (qn * eg).astype(BF16)
    k_dec_t = (kn * jnp.exp(glast_x - gcum_x)).T
    kn_t = kn.T.astype(BF16)
    head_of_lane = _iota((1, GDN_WIDTH), 1) // GDN_DK
    eye = (ri == ci).astype(F32)
    lower_left = []
    b = 1
    while b < CHUNK:
        in_block = (ri // (2 * b)) == (ci // (2 * b))
        lower_left.append(jnp.logical_and(in_block, (ri // b) % 2 > (ci // b) % 2))
        b *= 2

    heads = range(GDN_HEADS)
    hms = [head_of_lane == h for h in heads]
    gammas = [jnp.exp2(jnp.where(incl, gcum2[:, SMALL_A + h:SMALL_A + h + 1] - gcum2_t[SMALL_A + h:SMALL_A + h + 1, :],
                                 NEG_BIG)) for h in heads]
    kks = [_dot(jnp.where(hms[h], kb, 0.0).astype(BF16), kn_t) for h in heads]
    mlows = [jnp.where(strict, kks[h] * gammas[h], 0.0) for h in heads]
    ps = [eye - jnp.where(lower_left[0], mlows[h], 0.0) for h in heads]
    for ll in lower_left[1:]:
        pbs = [p.astype(BF16) for p in ps]
        ts = [_dot(pbs[h], jnp.where(ll, mlows[h], 0.0).astype(BF16)).astype(BF16) for h in heads]
        ps = [ps[h] - _dot(ts[h], pbs[h]) for h in heads]
    sols = [_dot(ps[h].astype(BF16), rhs) for h in heads]
    u = sols[0][:, 0:256]
    w = sols[0][:, 256:512]
    for h in range(1, GDN_HEADS):
        u = jnp.where(hms[h], sols[h][:, 0:256], u)
        w = jnp.where(hms[h], sols[h][:, 256:512], w)
    qks = [_dot(jnp.where(hms[h], qn, 0.0).astype(BF16), kn_t) for h in heads]
    qk_heads = [jnp.where(incl, qks[h] * gammas[h], 0.0).astype(BF16) for h in heads]

    vnew_ref[...] = jnp.zeros((T, GDN_WIDTH), F32)
    chunk_of_col = _iota((1, T), 1) // CHUNK
    o_inter = []
    for c in range(T // CHUNK):
        r0 = c * CHUNK
        s_in = s_ref[...]
        s_b = s_in.astype(BF16)
        vnew_ref[r0:r0 + CHUNK, :] = u[r0:r0 + CHUNK, :] - _dot(w[r0:r0 + CHUNK, :].astype(BF16), s_b)
        o_inter.append(_dot(q_dec[r0:r0 + CHUNK, :], s_b))
        kd = jnp.where(chunk_of_col == c, k_dec_t, 0.0).astype(BF16)
        upd = _dot(kd, vnew_ref[...].astype(BF16))
        dec = jnp.exp(glast_x[r0:r0 + 1, :])
        s_ref[...] = jnp.where(same_head, s_in * dec + upd, 0.0)

    vnew = vnew_ref[...].astype(BF16)
    o = jnp.concatenate(o_inter, axis=0)
    for h in range(GDN_HEADS):
        o = o + jnp.where(head_of_lane == h, _dot(qk_heads[h], vnew), 0.0)
    ms = _dot_sel_right(o * o, head_ones) * (1.0 / GDN_DV)
    y = o * lax.rsqrt(ms + NORM_EPS) * ng_ref[...] * _silu(z)
    y_ref[...] = y.astype(BF16)


def _ssd_gdn_body(ssd_main, small, ssd_cw, ssd_cb, ssd_dtb, ssd_alog, ssd_dsk, ssd_ng,
                  gdn_main, gdn_cw, gdn_dtb, gdn_alog, gdn_ng,
                  y_ssd, y_gdn, ssd_xpad, ssd_state, gdn_xpad, gdn_state, gdn_vnew, *, T):
    @pl.when(pl.program_id(1) == 0)
    def _():
        _ssd_init(ssd_xpad, ssd_state)
        _gdn_init(gdn_xpad, gdn_state)

    _ssd_body(ssd_main, small, ssd_cw, ssd_cb, ssd_dtb, ssd_alog, ssd_dsk, ssd_ng, y_ssd, ssd_xpad, ssd_state, L=T)
    _gdn_body(gdn_main, small, gdn_cw, gdn_dtb, gdn_alog, gdn_ng, y_gdn, gdn_xpad, gdn_state, gdn_vnew, T=T)


def _ssd_gdn(ssd_main, small, ssd_cw, ssd_cb, ssd_dtb, ssd_alog, ssd_dsk, ssd_ng,
             gdn_main, gdn_cw, gdn_dtb, gdn_alog, gdn_ng, bsz, seq, T):
    T = min(T, seq)
    nt = seq // T
    row = lambda b, t: (b * nt + t, 0)
    fixed = lambda b, t: (0, 0)
    return pl.pallas_call(
        functools.partial(_ssd_gdn_body, T=T),
        grid=(bsz, nt),
        in_specs=[
            pl.BlockSpec((T, SSD_MAIN), row),
            pl.BlockSpec((T, LANES), row),
            pl.BlockSpec((CONV_K, SSD_CONV_DIM), fixed),
            pl.BlockSpec((1, SSD_CONV_DIM), fixed),
            pl.BlockSpec((1, LANES), fixed),
            pl.BlockSpec((1, LANES), fixed),
            pl.BlockSpec((1, SSD_WIDTH), fixed),
            pl.BlockSpec((1, SSD_WIDTH), fixed),
            pl.BlockSpec((T, GDN_MAIN), row),
            pl.BlockSpec((CONV_K, GDN_CONV_DIM), fixed),
            pl.BlockSpec((1, LANES), fixed),
            pl.BlockSpec((1, LANES), fixed),
            pl.BlockSpec((1, GDN_WIDTH), fixed),
        ],
        out_specs=[pl.BlockSpec((T, SSD_WIDTH), row), pl.BlockSpec((T, GDN_WIDTH), row)],
        out_shape=[jax.ShapeDtypeStruct((bsz * seq, SSD_WIDTH), BF16),
                   jax.ShapeDtypeStruct((bsz * seq, GDN_WIDTH), BF16)],
        scratch_shapes=[
            pltpu.VMEM((8 + T, SSD_CONV_DIM), F32),
            pltpu.VMEM((LANES, SSD_WIDTH), F32),
            pltpu.VMEM((8 + T, GDN_CONV_DIM), F32),
            pltpu.VMEM((GDN_WIDTH, GDN_WIDTH), F32),
            pltpu.VMEM((T, GDN_WIDTH), F32),
        ],
        compiler_params=_params(("parallel", "arbitrary")),
        name="ssd_gdn",
    )(ssd_main, small, ssd_cw, ssd_cb, ssd_dtb, ssd_alog, ssd_dsk, ssd_ng, gdn_main, gdn_cw, gdn_dtb, gdn_alog, gdn_ng)


def _mla_attn_body(qt_ref, k_ref, vt_ref, o_ref, *, tq, tk):
    i = pl.program_id(1)
    sub = tq // tk
    qts = [jnp.concatenate([qt_ref[0, d, h * HEAD_PAD:(h + 1) * HEAD_PAD, :] for d in range(qt_ref.shape[1])], axis=1)
           for h in range(MLA_HEADS)]

    def step(j, carry, diag_sub):
        kblk = k_ref[0, pl.ds(pl.multiple_of(j * tk, tk), tk), :]
        if diag_sub is not None:
            visible = ((_iota((tk, tq), 0) + diag_sub * tk) // CHUNK) <= (_iota((tk, tq), 1) // CHUNK)
        out = []
        for h0 in range(0, MLA_HEADS, MLA_INTERLEAVE):
            heads = range(h0, h0 + MLA_INTERLEAVE)
            ss = {h: _dot(kblk[:, h * HEAD_PAD:(h + 1) * HEAD_PAD], qts[h]) for h in heads}
            if diag_sub is not None:
                ss = {h: jnp.where(visible, ss[h], NEG_BIG) for h in heads}
            m_new = {h: jnp.maximum(carry[h][0], jnp.max(ss[h], axis=0, keepdims=True)) for h in heads}
            ps = {h: jnp.exp2(ss[h] - m_new[h]).astype(BF16) for h in heads}
            pv = {h: _dot(vt_ref[0, j, h * V_ROWS:(h + 1) * V_ROWS, :], ps[h]) for h in heads}
            out.extend((m_new[h], jnp.exp2(carry[h][0] - m_new[h]) * carry[h][1] + pv[h]) for h in heads)
        return tuple(out)

    init = tuple((jnp.full((1, tq), NEG_BIG, F32), jnp.zeros((V_ROWS, tq), F32)) for _ in range(MLA_HEADS))
    carry = lax.fori_loop(0, i * sub, functools.partial(step, diag_sub=None), init)
    for d in range(sub):
        carry = step(i * sub + d, carry, d)
    outs = []
    for h in range(MLA_HEADS):
        acc = carry[h][1]
        normed = acc / acc[MLA_V:MLA_V + 1, :]
        outs.append(jnp.concatenate([normed, jnp.zeros((HEAD_PAD - V_ROWS, tq), F32)], axis=0).T)
    o_ref[...] = jnp.concatenate(outs, axis=1).astype(BF16)


def _mla_attn(qt, k, vt, bsz, seq, tq, tk):
    tq = min(tq, seq)
    nq = seq // tq
    width = MLA_HEADS * HEAD_PAD
    return pl.pallas_call(
        functools.partial(_mla_attn_body, tq=tq, tk=tk),
        grid=(bsz, nq),
        in_specs=[
            pl.BlockSpec((1, tq // qt.shape[3], width, qt.shape[3]), lambda b, i: (b, i, 0, 0)),
            pl.BlockSpec((1, seq, width), lambda b, i: (b, 0, 0)),
            pl.BlockSpec((1, seq // tk, MLA_HEADS * V_ROWS, tk), lambda b, i: (b, 0, 0, 0)),
        ],
        out_specs=pl.BlockSpec((tq, width), lambda b, i: (b * nq + i, 0)),
        out_shape=jax.ShapeDtypeStruct((bsz * seq, width), BF16),
        compiler_params=_params(("parallel", "arbitrary")),
        name="mla_attn",
    )(qt, k.reshape(bsz, seq, width), vt)


def _memkv_body(mem_ref, g_ref, w_ref, kt_ref, v_ref):
    mn = _rms(mem_ref[...], g_ref[...]).astype(BF16)
    kv = _dot(mn, w_ref[...])
    kt_ref[0] = kv[:, 0:D_MODEL].T.astype(BF16)
    v_ref[0] = kv[:, D_MODEL:2 * D_MODEL].astype(BF16)


def _memkv(mem2d, g, w, bsz):
    return pl.pallas_call(
        _memkv_body,
        grid=(bsz,),
        in_specs=[
            pl.BlockSpec((N_MEM, D_MODEL), lambda b: (b, 0)),
            pl.BlockSpec((1, D_MODEL), lambda b: (0, 0)),
            pl.BlockSpec((D_MODEL, 2 * D_MODEL), lambda b: (0, 0)),
        ],
        out_specs=[
            pl.BlockSpec((1, D_MODEL, N_MEM), lambda b: (b, 0, 0)),
            pl.BlockSpec((1, N_MEM, D_MODEL), lambda b: (b, 0, 0)),
        ],
        out_shape=[
            jax.ShapeDtypeStruct((bsz, D_MODEL, N_MEM), BF16),
            jax.ShapeDtypeStruct((bsz, N_MEM, D_MODEL), BF16),
        ],
        compiler_params=_params(("parallel",)),
        name="memkv",
    )(mem2d, g, w)


def _post_body(h_ref, ys_ref, ym_ref, yg_ref, wos_ref, wom_ref, wog_ref, gmix_ref, gxa_ref,
               wq_ref, kt_ref, v_ref, wo_ref, gpost_ref, o_ref):
    mix = _dot(ys_ref[...], wos_ref[...]) + _dot(ym_ref[...], wom_ref[...]) + _dot(yg_ref[...], wog_ref[...])
    h1 = h_ref[...] + _rms(mix, gmix_ref[...])
    u = _rms(h1, gxa_ref[...]).astype(BF16)
    q = (_dot(u, wq_ref[...]) * (XA_HEAD_DIM ** -0.5)).astype(BF16)
    heads = range(XA_HEADS)
    w = XA_HEAD_DIM
    ss = [_dot(q[:, hd * w:(hd + 1) * w], kt_ref[0, hd * w:(hd + 1) * w, :]) for hd in heads]
    ps = [jnp.exp(s - jnp.max(s, axis=-1, keepdims=True)) for s in ss]
    os_ = [_dot(ps[hd].astype(BF16), v_ref[0, :, hd * w:(hd + 1) * w]) for hd in heads]
    outs = [(os_[hd] / jnp.sum(ps[hd], axis=-1, keepdims=True)).astype(BF16) for hd in heads]
    xa = _dot(jnp.concatenate(outs, axis=1), wo_ref[...])
    o_ref[...] = h1 + _rms(xa, gpost_ref[...])


def _post(h, ys, ym, yg, wos, wom, wog, gmix, gxa, wq, kt, v, wo, gpost, bsz, seq, tm):
    tm = min(tm, seq)
    nt = seq // tm
    row = lambda b, t: (b * nt + t, 0)
    fixed = lambda b, t: (0, 0)
    full = lambda a: pl.BlockSpec(a.shape, fixed)
    return pl.pallas_call(
        _post_body,
        grid=(bsz, nt),
        in_specs=[
            pl.BlockSpec((tm, D_MODEL), row),
            pl.BlockSpec((tm, ys.shape[1]), row),
            pl.BlockSpec((tm, ym.shape[1]), row),
            pl.BlockSpec((tm, yg.shape[1]), row),
            full(wos), full(wom), full(wog), full(gmix), full(gxa), full(wq),
            pl.BlockSpec((1, D_MODEL, N_MEM), lambda b, t: (b, 0, 0)),
            pl.BlockSpec((1, N_MEM, D_MODEL), lambda b, t: (b, 0, 0)),
            full(wo), full(gpost),
        ],
        out_specs=pl.BlockSpec((tm, D_MODEL), row),
        out_shape=jax.ShapeDtypeStruct((bsz * seq, D_MODEL), F32),
        compiler_params=_params(("parallel", "arbitrary")),
        name="mix_out_xattn",
    )(h, ys, ym, yg, wos, wom, wog, gmix, gxa, wq, kt, v, wo, gpost)


def _pad_lanes(vals, offset, width=LANES):
    n = vals.shape[-1]
    pad = [(0, 0)] * (vals.ndim - 1) + [(offset, width - offset - n)]
    return jnp.pad(vals, pad)[..., None, :]


def _prep_params(norm_g, ffn_w_up, ffn_w_down, w_in, ssd_conv_w, ssd_conv_b, ssd_dt_bias, ssd_a_log,
                 ssd_d, ssd_norm_g, mla_q_norm_g, mla_w_uq, mla_kv_norm_g, mla_w_ukv, gdn_conv_w,
                 gdn_dt_bias, gdn_a_log, gdn_norm_g, w_out, xa_w_q, xa_w_kv, xa_w_o):
    depth = norm_g.shape[0]
    vone = np.zeros((1, MLA_HEADS * HEAD_PAD), np.float32)
    vone[0, np.arange(MLA_HEADS) * HEAD_PAD + MLA_V] = 1.0
    return dict(
        norm_g=norm_g[:, :, None, :],
        w_up=ffn_w_up.astype(BF16),
        wd=ffn_w_down.astype(BF16),
        w_in=_gather_cols(w_in, _IN_PERM).astype(BF16),
        ssd_cw=ssd_conv_w,
        ssd_cb=ssd_conv_b[:, None, :],
        ssd_dtb=_pad_lanes(ssd_dt_bias, SMALL_DT),
        ssd_alog=_pad_lanes(ssd_a_log, SMALL_DT),
        ssd_dsk=jnp.repeat(ssd_d, SSD_HEAD_DIM, axis=-1)[:, None, :],
        ssd_ng=ssd_norm_g[:, None, :],
        mla_gq=mla_q_norm_g[:, None, :],
        mla_gkv=mla_kv_norm_g[:, None, :],
        mla_wq=_gather_cols(mla_w_uq, _UQ_PERM).astype(BF16),
        mla_wkv=_gather_cols(mla_w_ukv, _UKV_PERM).astype(BF16),
        mla_vone=jnp.broadcast_to(jnp.asarray(vone), (depth,) + vone.shape),
        gdn_cw=gdn_conv_w,
        gdn_dtb=_pad_lanes(gdn_dt_bias, SMALL_A),
        gdn_alog=_pad_lanes(gdn_a_log, SMALL_A),
        gdn_ng=jnp.tile(gdn_norm_g, (1, GDN_HEADS))[:, None, :],
        wo_ssd=w_out[:, 0:SSD_WIDTH, :].astype(BF16),
        wo_mla=jnp.swapaxes(_gather_cols(jnp.swapaxes(w_out, 1, 2), _WOUT_MLA_ROWS), 1, 2).astype(BF16),
        wo_gdn=w_out[:, SSD_WIDTH + MLA_HEADS * MLA_V:, :].astype(BF16),
        xa_wq=xa_w_q.astype(BF16),
        xa_wkv=xa_w_kv.astype(BF16),
        xa_wo=xa_w_o.astype(BF16),
    )


def _rope_tables(positions):
    inv = 1.0 / (ROPE_THETA ** (jnp.arange(0, MLA_ROPE, 2, dtype=F32) / MLA_ROPE))
    ang = positions.astype(F32).reshape(-1, 1) * inv
    cos, sin = jnp.cos(ang), jnp.sin(ang)
    n = ang.shape[0]
    ones = jnp.ones((n, MLA_NOPE), F32)
    zeros_nope = jnp.zeros((n, MLA_NOPE), F32)
    zeros_pad = jnp.zeros((n, HEAD_PAD - MLA_NOPE - MLA_ROPE), F32)
    cos_t = jnp.concatenate([ones, cos, cos, zeros_pad], axis=1)
    sin_t = jnp.concatenate([zeros_nope, -sin, sin, zeros_pad], axis=1)
    return cos_t, sin_t


FFN_TM = 512
TOKEN_TM = 512
SEQ_TILE = 256
MLA_TK = 256
MLA_TQ = 512
MLA_INTERLEAVE = 4


def kernel(x, mem, positions, norm_g, ffn_w_up, ffn_w_down, w_in, ssd_conv_w, ssd_conv_b, ssd_dt_bias,
           ssd_a_log, ssd_d, ssd_norm_g, mla_q_norm_g, mla_w_uq, mla_kv_norm_g, mla_w_ukv, gdn_conv_w,
           gdn_dt_bias, gdn_a_log, gdn_norm_g, w_out, xa_w_q, xa_w_kv, xa_w_o):
    bsz, seq, _ = x.shape
    n = bsz * seq
    cos_t, sin_t = _rope_tables(positions)
    params = _prep_params(norm_g, ffn_w_up, ffn_w_down, w_in, ssd_conv_w, ssd_conv_b, ssd_dt_bias,
                          ssd_a_log, ssd_d, ssd_norm_g, mla_q_norm_g, mla_w_uq, mla_kv_norm_g, mla_w_ukv,
                          gdn_conv_w, gdn_dt_bias, gdn_a_log, gdn_norm_g, w_out, xa_w_q, xa_w_kv, xa_w_o)
    mem2d = mem.reshape(bsz * N_MEM, D_MODEL)

    def layer(h, p):
        g = p["norm_g"]
        h = _ffn(h, g[FFN1_PRE], p["w_up"][0], p["wd"][0], g[FFN1_POST], FFN_TM)
        c_ssd, c_gdn, c_small, qt, kk, vt = _inproj(h, g[MIX_PRE], p["w_in"], cos_t, sin_t, p["mla_gq"], p["mla_gkv"],
                                                    p["mla_wq"], p["mla_wkv"], p["mla_vone"], bsz, seq, TOKEN_TM)
        y_ssd, y_gdn = _ssd_gdn(c_ssd, c_small, p["ssd_cw"], p["ssd_cb"], p["ssd_dtb"], p["ssd_alog"], p["ssd_dsk"],
                                p["ssd_ng"], c_gdn, p["gdn_cw"], p["gdn_dtb"], p["gdn_alog"], p["gdn_ng"],
                                bsz, seq, SEQ_TILE)
        y_mla = _mla_attn(qt, kk, vt, bsz, seq, MLA_TQ, MLA_TK)
        mkt, mv = _memkv(mem2d, g[MEM_NORM], p["xa_wkv"], bsz)
        h = _post(h, y_ssd, y_mla, y_gdn, p["wo_ssd"], p["wo_mla"], p["wo_gdn"], g[MIX_POST], g[XA_PRE],
                  p["xa_wq"], mkt, mv, p["xa_wo"], g[XA_POST], bsz, seq, TOKEN_TM)
        h = _ffn(h, g[FFN2_PRE], p["w_up"][1], p["wd"][1], g[FFN2_POST], FFN_TM)
        return h

    h = x.reshape(n, D_MODEL)
    for i in range(norm_g.shape[0]):
        h = layer(h, jax.tree.map(lambda a: a[i], params))
    return h.reshape(bsz, seq, D_MODEL)
```

```python
import functools

import numpy as np
import jax
import jax.numpy as jnp
from jax import lax
from jax.experimental import pallas as pl
from jax.experimental.pallas import tpu as pltpu

F32 = jnp.float32
BF16 = jnp.bfloat16

D_MODEL = 1024
DEPTH = 4
CHUNK = 64
NORM_EPS = 1e-6
CONV_K = 4
D_FF = 2816
N_MEM = 256

SSD_HEADS = 8
SSD_HEAD_DIM = 64
SSD_WIDTH = 512
SSD_GROUPS = 2
SSD_STATE = 64
SSD_CONV_DIM = 768
SSD_IN = 1288

MLA_HEADS = 4
MLA_Q_LORA = 256
MLA_KV_LORA = 128
MLA_NOPE = 64
MLA_ROPE = 32
MLA_V = 64
MLA_IN = 416
ROPE_THETA = 10000.0
MLA_SCALE = float((MLA_NOPE + MLA_ROPE) ** -0.5)
LOG2E = float(np.log2(np.e))
MLA_Q_SCALE = MLA_SCALE * LOG2E

GDN_HEADS = 4
GDN_DK = 64
GDN_DV = 64
GDN_CONV_DIM = 768
GDN_WIDTH = 256
GDN_IN = 1032

XA_HEADS = 4
XA_HEAD_DIM = 256

(FFN1_PRE, FFN1_POST, MIX_PRE, MIX_POST, MEM_NORM, XA_PRE, XA_POST, FFN2_PRE, FFN2_POST) = range(9)

LANES = 128
HEAD_PAD = 128
V_ROWS = 80
NEG_BIG = -1e30
VMEM_LIMIT = 56 * 1024 * 1024

IN_COLS_PADDED = 3072
SSD_MAIN = 1280
GDN_MAIN = 1024
SMALL_DT, SMALL_B, SMALL_A = 0, 8, 12


def _in_perm():
    perm = -np.ones((IN_COLS_PADDED,), np.int64)
    perm[0:1280] = np.arange(0, 1280)
    mla0 = SSD_IN
    perm[1280:1664] = np.arange(mla0, mla0 + 384)
    kr0 = mla0 + 384
    half = MLA_ROPE // 2
    perm[1664 + 64:1664 + 96] = np.arange(kr0, kr0 + 32)
    perm[1792 + 64:1792 + 64 + half] = np.arange(kr0 + half, kr0 + 32)
    perm[1792 + 64 + half:1792 + 96] = np.arange(kr0, kr0 + half)
    gdn0 = SSD_IN + MLA_IN
    perm[1920:2944] = np.arange(gdn0, gdn0 + 1024)
    perm[2944 + SMALL_DT:2944 + SMALL_DT + 8] = np.arange(1280, 1288)
    perm[2944 + SMALL_B:2944 + SMALL_B + 4] = np.arange(gdn0 + 1024, gdn0 + 1028)
    perm[2944 + SMALL_A:2944 + SMALL_A + 4] = np.arange(gdn0 + 1028, gdn0 + 1032)
    return perm


_IN_PERM = _in_perm()


def _gather_cols(w, perm):
    valid = jnp.asarray(perm >= 0)
    idx = jnp.asarray(np.maximum(perm, 0))
    return jnp.where(valid, jnp.take(w, idx, axis=-1), 0.0)


def _uq_perms():
    a = -np.ones((MLA_HEADS * HEAD_PAD,), np.int64)
    b = -np.ones((MLA_HEADS * HEAD_PAD,), np.int64)
    hd = MLA_NOPE + MLA_ROPE
    half = MLA_ROPE // 2
    for h in range(MLA_HEADS):
        a[h * HEAD_PAD:h * HEAD_PAD + hd] = np.arange(h * hd, (h + 1) * hd)
        r0 = h * hd + MLA_NOPE
        b[h * HEAD_PAD + 64:h * HEAD_PAD + 64 + half] = np.arange(r0 + half, r0 + 32)
        b[h * HEAD_PAD + 64 + half:h * HEAD_PAD + 96] = np.arange(r0, r0 + half)
    return np.concatenate([a, b])


def _ukv_perms():
    k = -np.ones((MLA_HEADS * HEAD_PAD,), np.int64)
    v = -np.ones((MLA_HEADS * HEAD_PAD,), np.int64)
    hd = MLA_NOPE + MLA_V
    for h in range(MLA_HEADS):
        k[h * HEAD_PAD:h * HEAD_PAD + MLA_NOPE] = np.arange(h * hd, h * hd + MLA_NOPE)
        v[h * HEAD_PAD:h * HEAD_PAD + MLA_V] = np.arange(h * hd + MLA_NOPE, (h + 1) * hd)
    return np.concatenate([k, v])


_UQ_PERM = _uq_perms()
_UKV_PERM = _ukv_perms()


def _wout_mla_rows():
    rows = -np.ones((MLA_HEADS * HEAD_PAD,), np.int64)
    for h in range(MLA_HEADS):
        rows[h * HEAD_PAD:h * HEAD_PAD + MLA_V] = SSD_WIDTH + np.arange(h * MLA_V, (h + 1) * MLA_V)
    return rows


_WOUT_MLA_ROWS = _wout_mla_rows()


def _rms(x, g):
    return x * lax.rsqrt(jnp.mean(x * x, axis=-1, keepdims=True) + NORM_EPS) * g


def _silu(x):
    return x * jax.nn.sigmoid(x)


def _softplus(x):
    return jnp.maximum(x, 0.0) + jnp.log1p(jnp.exp(-jnp.abs(x)))


def _dot(a, b):
    return jnp.dot(a, b, preferred_element_type=F32)


def _split3(x):
    hi = x.astype(BF16)
    r1 = x - hi.astype(F32)
    mid = r1.astype(BF16)
    lo = (r1 - mid.astype(F32)).astype(BF16)
    return hi, mid, lo


def _dot_sel_right(x, sel):
    sel = sel.astype(BF16)
    hi, mid, lo = _split3(x)
    return _dot(hi, sel) + (_dot(mid, sel) + _dot(lo, sel))


def _dot_sel_left(sel, x):
    sel = sel.astype(BF16)
    hi, mid, lo = _split3(x)
    return _dot(sel, hi) + (_dot(sel, mid) + _dot(sel, lo))


def _iota(shape, dim):
    return lax.broadcasted_iota(jnp.int32, shape, dim)


def _params(sem):
    return pltpu.CompilerParams(dimension_semantics=sem, vmem_limit_bytes=VMEM_LIMIT)


def _ffn_body(h_ref, gpre_ref, wg_ref, wu_ref, wd_ref, gpost_ref, o_ref):
    h = h_ref[...]
    xn = _rms(h, gpre_ref[...]).astype(BF16)
    act = (_silu(_dot(xn, wg_ref[...])) * _dot(xn, wu_ref[...])).astype(BF16)
    o_ref[...] = h + 0.5 * _rms(_dot(act, wd_ref[...]), gpost_ref[...])


def _ffn(h, g_pre, w_up, w_down, g_post, tm):
    n = h.shape[0]
    tm = min(tm, n)
    resident = pl.Buffered(1)
    return pl.pallas_call(
        _ffn_body,
        grid=(n // tm,),
        in_specs=[
            pl.BlockSpec((tm, D_MODEL), lambda i: (i, 0)),
            pl.BlockSpec((1, D_MODEL), lambda i: (0, 0)),
            pl.BlockSpec((D_MODEL, D_FF), lambda i: (0, 0), pipeline_mode=resident),
            pl.BlockSpec((D_MODEL, D_FF), lambda i: (0, 1), pipeline_mode=resident),
            pl.BlockSpec((D_FF, D_MODEL), lambda i: (0, 0), pipeline_mode=resident),
            pl.BlockSpec((1, D_MODEL), lambda i: (0, 0)),
        ],
        out_specs=pl.BlockSpec((tm, D_MODEL), lambda i: (i, 0)),
        out_shape=jax.ShapeDtypeStruct((n, D_MODEL), F32),
        compiler_params=_params(("parallel",)),
        name="ffn",
    )(h, g_pre, w_up, w_up, w_down, g_post)


def _mla_project(c, cos_t, sin_t, gq, gkv, wq, wkv, vone):
    cq = _rms(c[:, 0:256], gq).astype(BF16)
    ckv = _rms(c[:, 256:384], gkv).astype(BF16)
    k_rot = c[:, 384:512] * cos_t + c[:, 512:640] * sin_t
    qq = _dot(cq, wq)
    kvv = _dot(ckv, wkv)
    width = MLA_HEADS * HEAD_PAD
    qs, ks = [], []
    for h in range(MLA_HEADS):
        lo, hi = h * HEAD_PAD, (h + 1) * HEAD_PAD
        qs.append((qq[:, lo:hi] * cos_t + qq[:, width + lo:width + hi] * sin_t) * MLA_Q_SCALE)
        ks.append(kvv[:, lo:hi] + k_rot)
    return jnp.concatenate(qs, axis=1), jnp.concatenate(ks, axis=1), kvv[:, width:2 * width] + vone


def _inproj_body(h_ref, g_ref, w_ref, cos_ref, sin_ref, gq_ref, gkv_ref, wq_ref, wkv_ref, vone_ref,
                 ssd_ref, gdn_ref, small_ref, qt_ref, k_ref, vt_ref):
    xn = _rms(h_ref[...], g_ref[...]).astype(BF16)
    y = _dot(xn, w_ref[...])
    ssd_ref[...] = y[:, 0:1280]
    gdn_ref[...] = y[:, 1920:2944]
    small_ref[...] = y[:, 2944:3072]
    q, k, v = _mla_project(y[:, 1280:1920], cos_ref[...], sin_ref[...], gq_ref[...], gkv_ref[...],
                           wq_ref[...], wkv_ref[...], vone_ref[...])
    qt_ref[0, 0] = q.T.astype(BF16)
    k_ref[...] = k.astype(BF16)
    vt_full = v.T.astype(BF16)
    vt = jnp.concatenate([vt_full[h * HEAD_PAD:h * HEAD_PAD + V_ROWS, :] for h in range(MLA_HEADS)], axis=0)
    for d in range(vt_ref.shape[1]):
        vt_ref[0, d] = vt[:, d * MLA_TK:(d + 1) * MLA_TK]


def _inproj(h, g, w, cos_t, sin_t, gq, gkv, wq, wkv, vone, bsz, seq, tm):
    tm = min(tm, seq)
    nt = seq // tm
    n = bsz * seq
    width = MLA_HEADS * HEAD_PAD
    row = lambda i: (i, 0)
    fixed = lambda i: (0, 0)
    blocked = lambda i: (i // nt, i % nt, 0, 0)
    return pl.pallas_call(
        _inproj_body,
        grid=(n // tm,),
        in_specs=[
            pl.BlockSpec((tm, D_MODEL), row),
            pl.BlockSpec((1, D_MODEL), fixed),
            pl.BlockSpec((D_MODEL, IN_COLS_PADDED), fixed),
            pl.BlockSpec((tm, LANES), row),
            pl.BlockSpec((tm, LANES), row),
            pl.BlockSpec((1, MLA_Q_LORA), fixed),
            pl.BlockSpec((1, MLA_KV_LORA), fixed),
            pl.BlockSpec((MLA_Q_LORA, 2 * width), fixed),
            pl.BlockSpec((MLA_KV_LORA, 2 * width), fixed),
            pl.BlockSpec((1, width), fixed),
        ],
        out_specs=[
            pl.BlockSpec((tm, SSD_MAIN), row),
            pl.BlockSpec((tm, GDN_MAIN), row),
            pl.BlockSpec((tm, LANES), row),
            pl.BlockSpec((1, 1, width, tm), blocked),
            pl.BlockSpec((tm, width), row),
            pl.BlockSpec((1, tm // MLA_TK, MLA_HEADS * V_ROWS, MLA_TK), blocked),
        ],
        out_shape=[
            jax.ShapeDtypeStruct((n, SSD_MAIN), F32),
            jax.ShapeDtypeStruct((n, GDN_MAIN), F32),
            jax.ShapeDtypeStruct((n, LANES), F32),
            jax.ShapeDtypeStruct((bsz, nt, width, tm), BF16),
            jax.ShapeDtypeStruct((n, width), BF16),
            jax.ShapeDtypeStruct((bsz, seq // MLA_TK, MLA_HEADS * V_ROWS, MLA_TK), BF16),
        ],
        compiler_params=_params(("parallel",)),
        name="inproj",
    )(h, g, w, cos_t, sin_t, gq, gkv, wq, wkv, vone)


def _causal_conv(x, xpad_ref, cw_ref, tile):
    xpad_ref[8:8 + tile, :] = x
    acc = cw_ref[CONV_K - 1:CONV_K, :] * x
    for k in range(CONV_K - 1):
        off = 8 - (CONV_K - 1) + k
        acc = acc + cw_ref[k:k + 1, :] * xpad_ref[off:off + tile, :]
    xpad_ref[0:8, :] = x[tile - 8:tile, :]
    return acc


def _ssd_init(xpad_ref, st_ref):
    xpad_ref[0:8, :] = jnp.zeros((8, SSD_CONV_DIM), F32)
    st_ref[...] = jnp.zeros_like(st_ref)


def _ssd_body(main_ref, small_ref, cw_ref, cb_ref, dtb_ref, alog_ref, dsk_ref, ng_ref,
              y_ref, xpad_ref, st_ref, sel_ref, *, L):
    z = main_ref[:, 0:SSD_WIDTH]
    xbc = main_ref[:, SSD_WIDTH:SSD_MAIN]
    xa = _silu(_causal_conv(xbc, xpad_ref, cw_ref, L) + cb_ref[...])
    xs = xa[:, 0:512]
    bm = xa[:, 512:640]
    cm = xa[:, 640:768]

    dt = _softplus(small_ref[...] + dtb_ref[...])
    dta = dt * (-jnp.exp(alog_ref[...]))
    ri = _iota((L, L), 0)
    ci = _iota((L, L), 1)
    causal = ri >= ci
    acum = _dot_sel_left(sel_ref["causal"][...], dta)
    expand = sel_ref["ssd_expand"][...]
    dt_x = _dot_sel_right(dt, expand)
    acum_x = _dot_sel_right(acum, expand)
    acum2 = acum * LOG2E
    acum2_t = acum2.T
    bm_t = bm.T.astype(BF16)
    alast_x = acum_x[L - 1:L, :]
    xdt = xs * dt_x

    state_in = st_ref[...]
    y_off = _dot(cm.astype(BF16), state_in.astype(BF16)) * jnp.exp(acum_x)
    w_end = (xdt * jnp.exp(alast_x - acum_x)).astype(BF16)
    chunk_state = _dot(bm_t, w_end)
    own_group = (_iota((LANES, SSD_WIDTH), 0) // SSD_STATE) == (_iota((LANES, SSD_WIDTH), 1) // 256)
    st_ref[...] = jnp.where(own_group, state_in * jnp.exp(alast_x) + chunk_state, 0.0)

    lane = _iota((L, LANES), 1)
    group_of_lane = _iota((1, LANES), 1) // SSD_STATE
    heads = range(SSD_HEADS)
    hpg = SSD_HEADS // SSD_GROUPS
    cbs = [_dot(jnp.where(group_of_lane == g, cm, 0.0).astype(BF16), bm_t) for g in range(SSD_GROUPS)]
    segs = [jnp.where(causal, acum2[:, hh:hh + 1] - acum2_t[hh:hh + 1, :], NEG_BIG) for hh in heads]
    gs = [(cbs[hh // hpg] * jnp.exp2(segs[hh])).astype(BF16) for hh in heads]
    xps = [xdt[:, slot * LANES:(slot + 1) * LANES].astype(BF16) for slot in range(SSD_HEADS // 2)]
    res = [_dot(gs[hh], xps[hh // 2]) for hh in heads]
    parts = [jnp.where(lane < SSD_HEAD_DIM, res[2 * slot], res[2 * slot + 1]) for slot in range(SSD_HEADS // 2)]
    y = jnp.concatenate(parts, axis=1) + y_off + xs * dsk_ref[...]
    y = y * _silu(z)
    outs = []
    for g in range(SSD_GROUPS):
        yg = y[:, g * 256:(g + 1) * 256]
        outs.append(yg * lax.rsqrt(jnp.mean(yg * yg, axis=-1, keepdims=True) + NORM_EPS))
    y_ref[...] = (jnp.concatenate(outs, axis=1) * ng_ref[...]).astype(BF16)


def _gdn_init(xpad_ref, s_ref):
    xpad_ref[0:8, :] = jnp.zeros((8, GDN_CONV_DIM), F32)
    s_ref[...] = jnp.zeros_like(s_ref)


def _gdn_body(main_ref, small_ref, cw_ref, dtb_ref, alog_ref, ng_ref,
              y_ref, xpad_ref, s_ref, vnew_ref, sel_ref, *, T):
    qkv_raw = main_ref[:, 0:GDN_CONV_DIM]
    z = main_ref[:, GDN_CONV_DIM:GDN_MAIN]
    xa = _silu(_causal_conv(qkv_raw, xpad_ref, cw_ref, T))
    q = xa[:, 0:256]
    k = xa[:, 256:512]
    v = xa[:, 512:768]

    same_head = (_iota((256, 256), 0) // GDN_DK) == (_iota((256, 256), 1) // GDN_DK)
    head_ones = sel_ref["head_ones"][...]
    qn = q * lax.rsqrt(_dot_sel_right(q * q, head_ones) + NORM_EPS) * (GDN_DK ** -0.5)
    kn = k * lax.rsqrt(_dot_sel_right(k * k, head_ones) + NORM_EPS)

    sm = small_ref[...]
    beta = jax.nn.sigmoid(sm)
    gl = -jnp.exp(alog_ref[...]) * _softplus(sm + dtb_ref[...])
    ri = _iota((T, T), 0)
    ci = _iota((T, T), 1)
    same_chunk = (ri // CHUNK) == (ci // CHUNK)
    incl = jnp.logical_and(same_chunk, ri >= ci)
    strict = jnp.logical_and(same_chunk, ri > ci)
    gcum = _dot_sel_left(sel_ref["incl"][...], gl)
    beta_x = _dot_sel_right(beta, sel_ref["beta_expand"][...])
    gcum_x = _dot_sel_right(gcum, sel_ref["decay_expand"][...])
    glast_x = _dot_sel_left(sel_ref["chunk_last"][...], gcum_x)
    gcum2 = gcum * LOG2E
    gcum2_t = gcum2.T

    kb = kn * beta_x
    eg = jnp.exp(gcum_x)
    rhs = jnp.concatenate([v * beta_x, kb * eg], axis=1).astype(BF16)
    q_dec = (qn * eg).astype(BF16)
    k_dec_t = (kn * jnp.exp(glast_x - gcum_x)).T
    kn_t = kn.T.astype(BF16)
    head_of_lane = _iota((1, GDN_WIDTH), 1) // GDN_DK
    eye = (ri == ci).astype(F32)
    lower_left = []
    b = 1
    while b < CHUNK:
        in_block = (ri // (2 * b)) == (ci // (2 * b))
        lower_left.append(jnp.logical_and(in_block, (ri // b) % 2 > (ci // b) % 2))
        b *= 2

    heads = range(GDN_HEADS)
    hms = [head_of_lane == h for h in heads]
    gammas = [jnp.exp2(jnp.where(incl, gcum2[:, SMALL_A + h:SMALL_A + h + 1] - gcum2_t[SMALL_A + h:SMALL_A + h + 1, :],
                                 NEG_BIG)) for h in heads]
    kks = [_dot(jnp.where(hms[h], kb, 0.0).astype(BF16), kn_t) for h in heads]
    mlows = [jnp.where(strict, kks[h] * gammas[h], 0.0) for h in heads]
    ps = [eye - jnp.where(lower_left[0], mlows[h], 0.0) for h in heads]
    for ll in lower_left[1:]:
        pbs = [p.astype(BF16) for p in ps]
        ts = [_dot(pbs[h], jnp.where(ll, mlows[h], 0.0).astype(BF16)).astype(BF16) for h in heads]
        ps = [ps[h] - _dot(ts[h], pbs[h]) for h in heads]
    sols = [_dot(ps[h].astype(BF16), rhs) for h in heads]
    u = sols[0][:, 0:256]
    w = sols[0][:, 256:512]
    for h in range(1, GDN_HEADS):
        u = jnp.where(hms[h], sols[h][:, 0:256], u)
        w = jnp.where(hms[h], sols[h][:, 256:512], w)
    qks = [_dot(jnp.where(hms[h], qn, 0.0).astype(BF16), kn_t) for h in heads]
    qk_heads = [jnp.where(incl, qks[h] * gammas[h], 0.0).astype(BF16) for h in heads]

    vnew_ref[...] = jnp.zeros((T, GDN_WIDTH), F32)
    chunk_of_col = _iota((1, T), 1) // CHUNK
    o_inter = []
    for c in range(T // CHUNK):
        r0 = c * CHUNK
        s_in = s_ref[...]
        s_b = s_in.astype(BF16)
        vnew_ref[r0:r0 + CHUNK, :] = u[r0:r0 + CHUNK, :] - _dot(w[r0:r0 + CHUNK, :].astype(BF16), s_b)
        o_inter.append(_dot(q_dec[r0:r0 + CHUNK, :], s_b))
        kd = jnp.where(chunk_of_col == c, k_dec_t, 0.0).astype(BF16)
        upd = _dot(kd, vnew_ref[...].astype(BF16))
        dec = jnp.exp(glast_x[r0:r0 + 1, :])
        s_ref[...] = jnp.where(same_head, s_in * dec + upd, 0.0)

    vnew = vnew_ref[...].astype(BF16)
    o = jnp.concatenate(o_inter, axis=0)
    for h in range(GDN_HEADS):
        o = o + jnp.where(head_of_lane == h, _dot(qk_heads[h], vnew), 0.0)
    ms = _dot_sel_right(o * o, head_ones) * (1.0 / GDN_DV)
    y = o * lax.rsqrt(ms + NORM_EPS) * ng_ref[...] * _silu(z)
    y_ref[...] = y.astype(BF16)


_SELECTORS = ("causal", "incl", "chunk_last", "head_ones", "ssd_expand", "beta_expand", "decay_expand")


def _selector_shapes(T):
    return dict(causal=(T, T), incl=(T, T), chunk_last=(T, T), head_ones=(GDN_WIDTH, GDN_WIDTH),
                ssd_expand=(LANES, SSD_WIDTH), beta_expand=(LANES, GDN_WIDTH), decay_expand=(LANES, GDN_WIDTH))


def _selectors_init(sel_ref, T):
    ri = _iota((T, T), 0)
    ci = _iota((T, T), 1)
    same_chunk = (ri // CHUNK) == (ci // CHUNK)
    hr = _iota((GDN_WIDTH, GDN_WIDTH), 0) // GDN_DK
    hc = _iota((GDN_WIDTH, GDN_WIDTH), 1) // GDN_DK
    er = _iota((LANES, GDN_WIDTH), 0)
    ec = _iota((LANES, GDN_WIDTH), 1) // GDN_DK
    masks = dict(
        causal=ri >= ci,
        incl=jnp.logical_and(same_chunk, ri >= ci),
        chunk_last=ci == (ri // CHUNK) * CHUNK + (CHUNK - 1),
        head_ones=hr == hc,
        ssd_expand=_iota((LANES, SSD_WIDTH), 0) == _iota((LANES, SSD_WIDTH), 1) // SSD_HEAD_DIM,
        beta_expand=er == ec + SMALL_B,
        decay_expand=er == ec + SMALL_A,
    )
    for name in _SELECTORS:
        sel_ref[name][...] = masks[name].astype(F32).astype(BF16)


def _ssd_gdn_body(ssd_main, small, ssd_cw, ssd_cb, ssd_dtb, ssd_alog, ssd_dsk, ssd_ng,
                  gdn_main, gdn_cw, gdn_dtb, gdn_alog, gdn_ng,
                  y_ssd, y_gdn, ssd_xpad, ssd_state, gdn_xpad, gdn_state, gdn_vnew, *sel_refs, T):
    sel_ref = dict(zip(_SELECTORS, sel_refs))

    @pl.when(pl.program_id(1) == 0)
    def _():
        _ssd_init(ssd_xpad, ssd_state)
        _gdn_init(gdn_xpad, gdn_state)
        _selectors_init(sel_ref, T)

    _ssd_body(ssd_main, small, ssd_cw, ssd_cb, ssd_dtb, ssd_alog, ssd_dsk, ssd_ng, y_ssd, ssd_xpad, ssd_state,
              sel_ref, L=T)
    _gdn_body(gdn_main, small, gdn_cw, gdn_dtb, gdn_alog, gdn_ng, y_gdn, gdn_xpad, gdn_state, gdn_vnew,
              sel_ref, T=T)


def _ssd_gdn(ssd_main, small, ssd_cw, ssd_cb, ssd_dtb, ssd_alog, ssd_dsk, ssd_ng,
             gdn_main, gdn_cw, gdn_dtb, gdn_alog, gdn_ng, bsz, seq, T):
    T = min(T, seq)
    nt = seq // T
    row = lambda b, t: (b * nt + t, 0)
    fixed = lambda b, t: (0, 0)
    return pl.pallas_call(
        functools.partial(_ssd_gdn_body, T=T),
        grid=(bsz, nt),
        in_specs=[
            pl.BlockSpec((T, SSD_MAIN), row),
            pl.BlockSpec((T, LANES), row),
            pl.BlockSpec((CONV_K, SSD_CONV_DIM), fixed),
            pl.BlockSpec((1, SSD_CONV_DIM), fixed),
            pl.BlockSpec((1, LANES), fixed),
            pl.BlockSpec((1, LANES), fixed),
            pl.BlockSpec((1, SSD_WIDTH), fixed),
            pl.BlockSpec((1, SSD_WIDTH), fixed),
            pl.BlockSpec((T, GDN_MAIN), row),
            pl.BlockSpec((CONV_K, GDN_CONV_DIM), fixed),
            pl.BlockSpec((1, LANES), fixed),
            pl.BlockSpec((1, LANES), fixed),
            pl.BlockSpec((1, GDN_WIDTH), fixed),
        ],
        out_specs=[pl.BlockSpec((T, SSD_WIDTH), row), pl.BlockSpec((T, GDN_WIDTH), row)],
        out_shape=[jax.ShapeDtypeStruct((bsz * seq, SSD_WIDTH), BF16),
                   jax.ShapeDtypeStruct((bsz * seq, GDN_WIDTH), BF16)],
        scratch_shapes=[
            pltpu.VMEM((8 + T, SSD_CONV_DIM), F32),
            pltpu.VMEM((LANES, SSD_WIDTH), F32),
            pltpu.VMEM((8 + T, GDN_CONV_DIM), F32),
            pltpu.VMEM((GDN_WIDTH, GDN_WIDTH), F32),
            pltpu.VMEM((T, GDN_WIDTH), F32),
        ] + [pltpu.VMEM(_selector_shapes(T)[name], BF16) for name in _SELECTORS],
        compiler_params=_params(("parallel", "arbitrary")),
        name="ssd_gdn",
    )(ssd_main, small, ssd_cw, ssd_cb, ssd_dtb, ssd_alog, ssd_dsk, ssd_ng, gdn_main, gdn_cw, gdn_dtb, gdn_alog, gdn_ng)


def _mla_attn_body(qt_ref, k_ref, vt_ref, o_ref, *, tq, tk):
    i = pl.program_id(1)
    sub = tq // tk
    qts = [jnp.concatenate([qt_ref[0, d, h * HEAD_PAD:(h + 1) * HEAD_PAD, :] for d in range(qt_ref.shape[1])], axis=1)
           for h in range(MLA_HEADS)]

    def step(j, carry, diag_sub):
        kblk = k_ref[0, pl.ds(pl.multiple_of(j * tk, tk), tk), :]
        if diag_sub is not None:
            visible = ((_iota((tk, tq), 0) + diag_sub * tk) // CHUNK) <= (_iota((tk, tq), 1) // CHUNK)
        out = []
        for h0 in range(0, MLA_HEADS, MLA_INTERLEAVE):
            heads = range(h0, h0 + MLA_INTERLEAVE)
            ss = {h: _dot(kblk[:, h * HEAD_PAD:(h + 1) * HEAD_PAD], qts[h]) for h in heads}
            if diag_sub is not None:
                ss = {h: jnp.where(visible, ss[h], NEG_BIG) for h in heads}
            m_new = {h: jnp.maximum(carry[h][0], jnp.max(ss[h], axis=0, keepdims=True)) for h in heads}
            ps = {h: jnp.exp2(ss[h] - m_new[h]).astype(BF16) for h in heads}
            pv = {h: _dot(vt_ref[0, j, h * V_ROWS:(h + 1) * V_ROWS, :], ps[h]) for h in heads}
            out.extend((m_new[h], jnp.exp2(carry[h][0] - m_new[h]) * carry[h][1] + pv[h]) for h in heads)
        return tuple(out)

    init = tuple((jnp.full((1, tq), NEG_BIG, F32), jnp.zeros((V_ROWS, tq), F32)) for _ in range(MLA_HEADS))
    carry = lax.fori_loop(0, i * sub, functools.partial(step, diag_sub=None), init)
    for d in range(sub):
        carry = step(i * sub + d, carry, d)
    outs = []
    for h in range(MLA_HEADS):
        acc = carry[h][1]
        normed = acc / acc[MLA_V:MLA_V + 1, :]
        outs.append(jnp.concatenate([normed, jnp.zeros((HEAD_PAD - V_ROWS, tq), F32)], axis=0).T)
    o_ref[...] = jnp.concatenate(outs, axis=1).astype(BF16)


def _mla_attn(qt, k, vt, bsz, seq, tq, tk):
    tq = min(tq, seq)
    nq = seq // tq
    width = MLA_HEADS * HEAD_PAD
    return pl.pallas_call(
        functools.partial(_mla_attn_body, tq=tq, tk=tk),
        grid=(bsz, nq),
        in_specs=[
            pl.BlockSpec((1, tq // qt.shape[3], width, qt.shape[3]), lambda b, i: (b, i, 0, 0)),
            pl.BlockSpec((1, seq, width), lambda b, i: (b, 0, 0)),
            pl.BlockSpec((1, seq // tk, MLA_HEADS * V_ROWS, tk), lambda b, i: (b, 0, 0, 0)),
        ],
        out_specs=pl.BlockSpec((tq, width), lambda b, i: (b * nq + i, 0)),
        out_shape=jax.ShapeDtypeStruct((bsz * seq, width), BF16),
        compiler_params=_params(("parallel", "arbitrary")),
        name="mla_attn",
    )(qt, k.reshape(bsz, seq, width), vt)


def _memkv_body(mem_ref, g_ref, w_ref, kt_ref, v_ref):
    mn = _rms(mem_ref[...], g_ref[...]).astype(BF16)
    kv = _dot(mn, w_ref[...])
    kt_ref[0] = kv[:, 0:D_MODEL].T.astype(BF16)
    v_ref[0] = kv[:, D_MODEL:2 * D_MODEL].astype(BF16)


def _memkv(mem2d, g, w, bsz):
    return pl.pallas_call(
        _memkv_body,
        grid=(bsz,),
        in_specs=[
            pl.BlockSpec((N_MEM, D_MODEL), lambda b: (b, 0)),
            pl.BlockSpec((1, D_MODEL), lambda b: (0, 0)),
            pl.BlockSpec((D_MODEL, 2 * D_MODEL), lambda b: (0, 0)),
        ],
        out_specs=[
            pl.BlockSpec((1, D_MODEL, N_MEM), lambda b: (b, 0, 0)),
            pl.BlockSpec((1, N_MEM, D_MODEL), lambda b: (b, 0, 0)),
        ],
        out_shape=[
            jax.ShapeDtypeStruct((bsz, D_MODEL, N_MEM), BF16),
            jax.ShapeDtypeStruct((bsz, N_MEM, D_MODEL), BF16),
        ],
        compiler_params=_params(("parallel",)),
        name="memkv",
    )(mem2d, g, w)


def _post_body(h_ref, ys_ref, ym_ref, yg_ref, wos_ref, wom_ref, wog_ref, gmix_ref, gxa_ref,
               wq_ref, kt_ref, v_ref, wo_ref, gpost_ref, o_ref):
    mix = _dot(ys_ref[...], wos_ref[...]) + _dot(ym_ref[...], wom_ref[...]) + _dot(yg_ref[...], wog_ref[...])
    h1 = h_ref[...] + _rms(mix, gmix_ref[...])
    u = _rms(h1, gxa_ref[...]).astype(BF16)
    q = (_dot(u, wq_ref[...]) * (XA_HEAD_DIM ** -0.5)).astype(BF16)
    heads = range(XA_HEADS)
    w = XA_HEAD_DIM
    ss = [_dot(q[:, hd * w:(hd + 1) * w], kt_ref[0, hd * w:(hd + 1) * w, :]) for hd in heads]
    ps = [jnp.exp(s - jnp.max(s, axis=-1, keepdims=True)) for s in ss]
    os_ = [_dot(ps[hd].astype(BF16), v_ref[0, :, hd * w:(hd + 1) * w]) for hd in heads]
    outs = [(os_[hd] / jnp.sum(ps[hd], axis=-1, keepdims=True)).astype(BF16) for hd in heads]
    xa = _dot(jnp.concatenate(outs, axis=1), wo_ref[...])
    o_ref[...] = h1 + _rms(xa, gpost_ref[...])


def _post(h, ys, ym, yg, wos, wom, wog, gmix, gxa, wq, kt, v, wo, gpost, bsz, seq, tm):
    tm = min(tm, seq)
    nt = seq // tm
    row = lambda b, t: (b * nt + t, 0)
    fixed = lambda b, t: (0, 0)
    full = lambda a: pl.BlockSpec(a.shape, fixed)
    return pl.pallas_call(
        _post_body,
        grid=(bsz, nt),
        in_specs=[
            pl.BlockSpec((tm, D_MODEL), row),
            pl.BlockSpec((tm, ys.shape[1]), row),
            pl.BlockSpec((tm, ym.shape[1]), row),
            pl.BlockSpec((tm, yg.shape[1]), row),
            full(wos), full(wom), full(wog), full(gmix), full(gxa), full(wq),
            pl.BlockSpec((1, D_MODEL, N_MEM), lambda b, t: (b, 0, 0)),
            pl.BlockSpec((1, N_MEM, D_MODEL), lambda b, t: (b, 0, 0)),
            full(wo), full(gpost),
        ],
        out_specs=pl.BlockSpec((tm, D_MODEL), row),
        out_shape=jax.ShapeDtypeStruct((bsz * seq, D_MODEL), F32),
        compiler_params=_params(("parallel", "arbitrary")),
        name="mix_out_xattn",
    )(h, ys, ym, yg, wos, wom, wog, gmix, gxa, wq, kt, v, wo, gpost)


def _pad_lanes(vals, offset, width=LANES):
    n = vals.shape[-1]
    pad = [(0, 0)] * (vals.ndim - 1) + [(offset, width - offset - n)]
    return jnp.pad(vals, pad)[..., None, :]


def _prep_params(norm_g, ffn_w_up, ffn_w_down, w_in, ssd_conv_w, ssd_conv_b, ssd_dt_bias, ssd_a_log,
                 ssd_d, ssd_norm_g, mla_q_norm_g, mla_w_uq, mla_kv_norm_g, mla_w_ukv, gdn_conv_w,
                 gdn_dt_bias, gdn_a_log, gdn_norm_g, w_out, xa_w_q, xa_w_kv, xa_w_o):
    depth = norm_g.shape[0]
    vone = np.zeros((1, MLA_HEADS * HEAD_PAD), np.float32)
    vone[0, np.arange(MLA_HEADS) * HEAD_PAD + MLA_V] = 1.0
    return dict(
        norm_g=norm_g[:, :, None, :],
        w_up=ffn_w_up.astype(BF16),
        wd=ffn_w_down.astype(BF16),
        w_in=_gather_cols(w_in, _IN_PERM).astype(BF16),
        ssd_cw=ssd_conv_w,
        ssd_cb=ssd_conv_b[:, None, :],
        ssd_dtb=_pad_lanes(ssd_dt_bias, SMALL_DT),
        ssd_alog=_pad_lanes(ssd_a_log, SMALL_DT),
        ssd_dsk=jnp.repeat(ssd_d, SSD_HEAD_DIM, axis=-1)[:, None, :],
        ssd_ng=ssd_norm_g[:, None, :],
        mla_gq=mla_q_norm_g[:, None, :],
        mla_gkv=mla_kv_norm_g[:, None, :],
        mla_wq=_gather_cols(mla_w_uq, _UQ_PERM).astype(BF16),
        mla_wkv=_gather_cols(mla_w_ukv, _UKV_PERM).astype(BF16),
        mla_vone=jnp.broadcast_to(jnp.asarray(vone), (depth,) + vone.shape),
        gdn_cw=gdn_conv_w,
        gdn_dtb=_pad_lanes(gdn_dt_bias, SMALL_A),
        gdn_alog=_pad_lanes(gdn_a_log, SMALL_A),
        gdn_ng=jnp.tile(gdn_norm_g, (1, GDN_HEADS))[:, None, :],
        wo_ssd=w_out[:, 0:SSD_WIDTH, :].astype(BF16),
        wo_mla=jnp.swapaxes(_gather_cols(jnp.swapaxes(w_out, 1, 2), _WOUT_MLA_ROWS), 1, 2).astype(BF16),
        wo_gdn=w_out[:, SSD_WIDTH + MLA_HEADS * MLA_V:, :].astype(BF16),
        xa_wq=xa_w_q.astype(BF16),
        xa_wkv=xa_w_kv.astype(BF16),
        xa_wo=xa_w_o.astype(BF16),
    )


def _rope_tables(positions):
    inv = 1.0 / (ROPE_THETA ** (jnp.arange(0, MLA_ROPE, 2, dtype=F32) / MLA_ROPE))
    ang = positions.astype(F32).reshape(-1, 1) * inv
    cos, sin = jnp.cos(ang), jnp.sin(ang)
    n = ang.shape[0]
    ones = jnp.ones((n, MLA_NOPE), F32)
    zeros_nope = jnp.zeros((n, MLA_NOPE), F32)
    zeros_pad = jnp.zeros((n, HEAD_PAD - MLA_NOPE - MLA_ROPE), F32)
    cos_t = jnp.concatenate([ones, cos, cos, zeros_pad], axis=1)
    sin_t = jnp.concatenate([zeros_nope, -sin, sin, zeros_pad], axis=1)
    return cos_t, sin_t


FFN_TM = 512
TOKEN_TM = 512
SEQ_TILE = 256
MLA_TK = 256
MLA_TQ = 512
MLA_INTERLEAVE = 4


def kernel(x, mem, positions, norm_g, ffn_w_up, ffn_w_down, w_in, ssd_conv_w, ssd_conv_b, ssd_dt_bias,
           ssd_a_log, ssd_d, ssd_norm_g, mla_q_norm_g, mla_w_uq, mla_kv_norm_g, mla_w_ukv, gdn_conv_w,
           gdn_dt_bias, gdn_a_log, gdn_norm_g, w_out, xa_w_q, xa_w_kv, xa_w_o):
    bsz, seq, _ = x.shape
    n = bsz * seq
    cos_t, sin_t = _rope_tables(positions)
    weights = (norm_g, ffn_w_up, ffn_w_down, w_in, ssd_conv_w, ssd_conv_b, ssd_dt_bias,
               ssd_a_log, ssd_d, ssd_norm_g, mla_q_norm_g, mla_w_uq, mla_kv_norm_g, mla_w_ukv,
               gdn_conv_w, gdn_dt_bias, gdn_a_log, gdn_norm_g, w_out, xa_w_q, xa_w_kv, xa_w_o)
    mem2d = mem.reshape(bsz * N_MEM, D_MODEL)

    def layer(h, p):
        g = p["norm_g"]
        h = _ffn(h, g[FFN1_PRE], p["w_up"][0], p["wd"][0], g[FFN1_POST], FFN_TM)
        c_ssd, c_gdn, c_small, qt, kk, vt = _inproj(h, g[MIX_PRE], p["w_in"], cos_t, sin_t, p["mla_gq"], p["mla_gkv"],
                                                    p["mla_wq"], p["mla_wkv"], p["mla_vone"], bsz, seq, TOKEN_TM)
        y_ssd, y_gdn = _ssd_gdn(c_ssd, c_small, p["ssd_cw"], p["ssd_cb"], p["ssd_dtb"], p["ssd_alog"], p["ssd_dsk"],
                                p["ssd_ng"], c_gdn, p["gdn_cw"], p["gdn_dtb"], p["gdn_alog"], p["gdn_ng"],
                                bsz, seq, SEQ_TILE)
        y_mla = _mla_attn(qt, kk, vt, bsz, seq, MLA_TQ, MLA_TK)
        mkt, mv = _memkv(mem2d, g[MEM_NORM], p["xa_wkv"], bsz)
        h = _post(h, y_ssd, y_mla, y_gdn, p["wo_ssd"], p["wo_mla"], p["wo_gdn"], g[MIX_POST], g[XA_PRE],
                  p["xa_wq"], mkt, mv, p["xa_wo"], g[XA_POST], bsz, seq, TOKEN_TM)
        h = _ffn(h, g[FFN2_PRE], p["w_up"][1], p["wd"][1], g[FFN2_POST], FFN_TM)
        return h

    h = x.reshape(n, D_MODEL)
    for i in range(norm_g.shape[0]):
        params = _prep_params(*(w[i:i + 1] for w in weights))
        h = layer(h, jax.tree.map(lambda a: a[0], params))
    return h.reshape(bsz, seq, D_MODEL)
```

```python
import functools

import numpy as np
import jax
import jax.numpy as jnp
from jax import lax
from jax.experimental import pallas as pl
from jax.experimental.pallas import tpu as pltpu

F32 = jnp.float32
BF16 = jnp.bfloat16

D_MODEL = 1024
DEPTH = 4
CHUNK = 64
NORM_EPS = 1e-6
CONV_K = 4
D_FF = 2816
N_MEM = 256

SSD_HEADS = 8
SSD_HEAD_DIM = 64
SSD_WIDTH = 512
SSD_GROUPS = 2
SSD_STATE = 64
SSD_CONV_DIM = 768
SSD_IN = 1288

MLA_HEADS = 4
MLA_Q_LORA = 256
MLA_KV_LORA = 128
MLA_NOPE = 64
MLA_ROPE = 32
MLA_V = 64
MLA_IN = 416
ROPE_THETA = 10000.0
MLA_SCALE = float((MLA_NOPE + MLA_ROPE) ** -0.5)
LOG2E = float(np.log2(np.e))
MLA_Q_SCALE = MLA_SCALE * LOG2E

GDN_HEADS = 4
GDN_DK = 64
GDN_DV = 64
GDN_CONV_DIM = 768
GDN_WIDTH = 256
GDN_IN = 1032

XA_HEADS = 4
XA_HEAD_DIM = 256

(FFN1_PRE, FFN1_POST, MIX_PRE, MIX_POST, MEM_NORM, XA_PRE, XA_POST, FFN2_PRE, FFN2_POST) = range(9)

LANES = 128
HEAD_PAD = 128
V_ROWS = 80
NEG_BIG = -1e30
VMEM_LIMIT = 56 * 1024 * 1024

IN_COLS_PADDED = 3072
SSD_MAIN = 1280
GDN_MAIN = 1024
SMALL_DT, SMALL_B, SMALL_A = 0, 8, 12


def _in_perm():
    perm = -np.ones((IN_COLS_PADDED,), np.int64)
    perm[0:1280] = np.arange(0, 1280)
    mla0 = SSD_IN
    perm[1280:1664] = np.arange(mla0, mla0 + 384)
    kr0 = mla0 + 384
    half = MLA_ROPE // 2
    perm[1664 + 64:1664 + 96] = np.arange(kr0, kr0 + 32)
    perm[1792 + 64:1792 + 64 + half] = np.arange(kr0 + half, kr0 + 32)
    perm[1792 + 64 + half:1792 + 96] = np.arange(kr0, kr0 + half)
    gdn0 = SSD_IN + MLA_IN
    perm[1920:2944] = np.arange(gdn0, gdn0 + 1024)
    perm[2944 + SMALL_DT:2944 + SMALL_DT + 8] = np.arange(1280, 1288)
    perm[2944 + SMALL_B:2944 + SMALL_B + 4] = np.arange(gdn0 + 1024, gdn0 + 1028)
    perm[2944 + SMALL_A:2944 + SMALL_A + 4] = np.arange(gdn0 + 1028, gdn0 + 1032)
    return perm


_IN_PERM = _in_perm()


def _gather_cols(w, perm):
    valid = jnp.asarray(perm >= 0)
    idx = jnp.asarray(np.maximum(perm, 0))
    return jnp.where(valid, jnp.take(w, idx, axis=-1), 0.0)


def _uq_perms():
    a = -np.ones((MLA_HEADS * HEAD_PAD,), np.int64)
    b = -np.ones((MLA_HEADS * HEAD_PAD,), np.int64)
    hd = MLA_NOPE + MLA_ROPE
    half = MLA_ROPE // 2
    for h in range(MLA_HEADS):
        a[h * HEAD_PAD:h * HEAD_PAD + hd] = np.arange(h * hd, (h + 1) * hd)
        r0 = h * hd + MLA_NOPE
        b[h * HEAD_PAD + 64:h * HEAD_PAD + 64 + half] = np.arange(r0 + half, r0 + 32)
        b[h * HEAD_PAD + 64 + half:h * HEAD_PAD + 96] = np.arange(r0, r0 + half)
    return np.concatenate([a, b])


def _ukv_perms():
    k = -np.ones((MLA_HEADS * HEAD_PAD,), np.int64)
    v = -np.ones((MLA_HEADS * HEAD_PAD,), np.int64)
    hd = MLA_NOPE + MLA_V
    for h in range(MLA_HEADS):
        k[h * HEAD_PAD:h * HEAD_PAD + MLA_NOPE] = np.arange(h * hd, h * hd + MLA_NOPE)
        v[h * HEAD_PAD:h * HEAD_PAD + MLA_V] = np.arange(h * hd + MLA_NOPE, (h + 1) * hd)
    return np.concatenate([k, v])


_UQ_PERM = _uq_perms()
_UKV_PERM = _ukv_perms()


def _wout_mla_rows():
    rows = -np.ones((MLA_HEADS * HEAD_PAD,), np.int64)
    for h in range(MLA_HEADS):
        rows[h * HEAD_PAD:h * HEAD_PAD + MLA_V] = SSD_WIDTH + np.arange(h * MLA_V, (h + 1) * MLA_V)
    return rows


_WOUT_MLA_ROWS = _wout_mla_rows()


def _rms(x, g):
    return x * lax.rsqrt(jnp.mean(x * x, axis=-1, keepdims=True) + NORM_EPS) * g


def _silu(x):
    return x * jax.nn.sigmoid(x)


def _softplus(x):
    return jnp.maximum(x, 0.0) + jnp.log1p(jnp.exp(-jnp.abs(x)))


def _dot(a, b):
    return jnp.dot(a, b, preferred_element_type=F32)


def _split3(x):
    hi = x.astype(BF16)
    r1 = x - hi.astype(F32)
    mid = r1.astype(BF16)
    lo = (r1 - mid.astype(F32)).astype(BF16)
    return hi, mid, lo


def _dot_sel_right(x, sel):
    sel = sel.astype(BF16)
    hi, mid, lo = _split3(x)
    return _dot(hi, sel) + (_dot(mid, sel) + _dot(lo, sel))


def _dot_sel_left(sel, x):
    sel = sel.astype(BF16)
    hi, mid, lo = _split3(x)
    return _dot(sel, hi) + (_dot(sel, mid) + _dot(sel, lo))


def _iota(shape, dim):
    return lax.broadcasted_iota(jnp.int32, shape, dim)


def _params(sem):
    return pltpu.CompilerParams(dimension_semantics=sem, vmem_limit_bytes=VMEM_LIMIT)


def _ffn_body(h_ref, gpre_ref, wg_ref, wu_ref, wd_ref, gpost_ref, o_ref):
    h = h_ref[...]
    xn = _rms(h, gpre_ref[...]).astype(BF16)
    act = (_silu(_dot(xn, wg_ref[...])) * _dot(xn, wu_ref[...])).astype(BF16)
    o_ref[...] = h + 0.5 * _rms(_dot(act, wd_ref[...]), gpost_ref[...])


def _ffn(h, g_pre, w_up, w_down, g_post, tm):
    n = h.shape[0]
    tm = min(tm, n)
    resident = pl.Buffered(1)
    return pl.pallas_call(
        _ffn_body,
        grid=(n // tm,),
        in_specs=[
            pl.BlockSpec((tm, D_MODEL), lambda i: (i, 0)),
            pl.BlockSpec((1, D_MODEL), lambda i: (0, 0)),
            pl.BlockSpec((D_MODEL, D_FF), lambda i: (0, 0), pipeline_mode=resident),
            pl.BlockSpec((D_MODEL, D_FF), lambda i: (0, 1), pipeline_mode=resident),
            pl.BlockSpec((D_FF, D_MODEL), lambda i: (0, 0), pipeline_mode=resident),
            pl.BlockSpec((1, D_MODEL), lambda i: (0, 0)),
        ],
        out_specs=pl.BlockSpec((tm, D_MODEL), lambda i: (i, 0)),
        out_shape=jax.ShapeDtypeStruct((n, D_MODEL), F32),
        compiler_params=_params(("parallel",)),
        name="ffn",
    )(h, g_pre, w_up, w_up, w_down, g_post)


def _mla_project(c, cos_t, sin_t, gq, gkv, wq, wkv, vone):
    cq = _rms(c[:, 0:256], gq).astype(BF16)
    ckv = _rms(c[:, 256:384], gkv).astype(BF16)
    k_rot = c[:, 384:512] * cos_t + c[:, 512:640] * sin_t
    qq = _dot(cq, wq)
    kvv = _dot(ckv, wkv)
    width = MLA_HEADS * HEAD_PAD
    qs, ks = [], []
    for h in range(MLA_HEADS):
        lo, hi = h * HEAD_PAD, (h + 1) * HEAD_PAD
        qs.append((qq[:, lo:hi] * cos_t + qq[:, width + lo:width + hi] * sin_t) * MLA_Q_SCALE)
        ks.append(kvv[:, lo:hi] + k_rot)
    return jnp.concatenate(qs, axis=1), jnp.concatenate(ks, axis=1), kvv[:, width:2 * width] + vone


def _inproj_body(h_ref, g_ref, w_ref, cos_ref, sin_ref, gq_ref, gkv_ref, wq_ref, wkv_ref, vone_ref,
                 ssd_ref, gdn_ref, small_ref, qt_ref, k_ref, vt_ref):
    xn = _rms(h_ref[...], g_ref[...]).astype(BF16)
    y = _dot(xn, w_ref[...])
    ssd_ref[...] = y[:, 0:1280]
    gdn_ref[...] = y[:, 1920:2944]
    small_ref[...] = y[:, 2944:3072]
    q, k, v = _mla_project(y[:, 1280:1920], cos_ref[...], sin_ref[...], gq_ref[...], gkv_ref[...],
                           wq_ref[...], wkv_ref[...], vone_ref[...])
    qt_ref[0, 0] = q.T.astype(BF16)
    k_ref[...] = k.astype(BF16)
    vt_full = v.T.astype(BF16)
    vt = jnp.concatenate([vt_full[h * HEAD_PAD:h * HEAD_PAD + V_ROWS, :] for h in range(MLA_HEADS)], axis=0)
    for d in range(vt_ref.shape[1]):
        vt_ref[0, d] = vt[:, d * MLA_TK:(d + 1) * MLA_TK]


def _inproj(h, g, w, cos_t, sin_t, gq, gkv, wq, wkv, vone, bsz, seq, tm):
    tm = min(tm, seq)
    nt = seq // tm
    n = bsz * seq
    width = MLA_HEADS * HEAD_PAD
    row = lambda i: (i, 0)
    fixed = lambda i: (0, 0)
    blocked = lambda i: (i // nt, i % nt, 0, 0)
    return pl.pallas_call(
        _inproj_body,
        grid=(n // tm,),
        in_specs=[
            pl.BlockSpec((tm, D_MODEL), row),
            pl.BlockSpec((1, D_MODEL), fixed),
            pl.BlockSpec((D_MODEL, IN_COLS_PADDED), fixed),
            pl.BlockSpec((tm, LANES), row),
            pl.BlockSpec((tm, LANES), row),
            pl.BlockSpec((1, MLA_Q_LORA), fixed),
            pl.BlockSpec((1, MLA_KV_LORA), fixed),
            pl.BlockSpec((MLA_Q_LORA, 2 * width), fixed),
            pl.BlockSpec((MLA_KV_LORA, 2 * width), fixed),
            pl.BlockSpec((1, width), fixed),
        ],
        out_specs=[
            pl.BlockSpec((tm, SSD_MAIN), row),
            pl.BlockSpec((tm, GDN_MAIN), row),
            pl.BlockSpec((tm, LANES), row),
            pl.BlockSpec((1, 1, width, tm), blocked),
            pl.BlockSpec((tm, width), row),
            pl.BlockSpec((1, tm // MLA_TK, MLA_HEADS * V_ROWS, MLA_TK), blocked),
        ],
        out_shape=[
            jax.ShapeDtypeStruct((n, SSD_MAIN), F32),
            jax.ShapeDtypeStruct((n, GDN_MAIN), F32),
            jax.ShapeDtypeStruct((n, LANES), F32),
            jax.ShapeDtypeStruct((bsz, nt, width, tm), BF16),
            jax.ShapeDtypeStruct((n, width), BF16),
            jax.ShapeDtypeStruct((bsz, seq // MLA_TK, MLA_HEADS * V_ROWS, MLA_TK), BF16),
        ],
        compiler_params=_params(("parallel",)),
        name="inproj",
    )(h, g, w, cos_t, sin_t, gq, gkv, wq, wkv, vone)


def _causal_conv(x, xpad_ref, cw_ref, tile):
    xpad_ref[8:8 + tile, :] = x
    acc = cw_ref[CONV_K - 1:CONV_K, :] * x
    for k in range(CONV_K - 1):
        off = 8 - (CONV_K - 1) + k
        acc = acc + cw_ref[k:k + 1, :] * xpad_ref[off:off + tile, :]
    xpad_ref[0:8, :] = x[tile - 8:tile, :]
    return acc


def _ssd_init(xpad_ref, st_ref):
    xpad_ref[0:8, :] = jnp.zeros((8, SSD_CONV_DIM), F32)
    st_ref[...] = jnp.zeros_like(st_ref)


def _ssd_body(main_ref, small_ref, cw_ref, cb_ref, dtb_ref, alog_ref, dsk_ref, ng_ref,
              y_ref, xpad_ref, st_ref, sel_ref, *, L):
    z = main_ref[:, 0:SSD_WIDTH]
    xbc = main_ref[:, SSD_WIDTH:SSD_MAIN]
    xa = _silu(_causal_conv(xbc, xpad_ref, cw_ref, L) + cb_ref[...])
    xs = xa[:, 0:512]
    bm = xa[:, 512:640]
    cm = xa[:, 640:768]

    dt = _softplus(small_ref[...] + dtb_ref[...])
    dta = dt * (-jnp.exp(alog_ref[...]))
    ri = _iota((L, L), 0)
    ci = _iota((L, L), 1)
    causal = ri >= ci
    acum = _dot_sel_left(sel_ref["causal"][...], dta)
    expand = sel_ref["ssd_expand"][...]
    dt_x = _dot_sel_right(dt, expand)
    acum_x = _dot_sel_right(acum, expand)
    acum2 = acum * LOG2E
    acum2_t = acum2.T
    bm_t = bm.T.astype(BF16)
    alast_x = acum_x[L - 1:L, :]
    xdt = xs * dt_x

    state_in = st_ref[...]
    y_off = _dot(cm.astype(BF16), state_in.astype(BF16)) * jnp.exp(acum_x)
    w_end = (xdt * jnp.exp(alast_x - acum_x)).astype(BF16)
    chunk_state = _dot(bm_t, w_end)
    own_group = (_iota((LANES, SSD_WIDTH), 0) // SSD_STATE) == (_iota((LANES, SSD_WIDTH), 1) // 256)
    st_ref[...] = jnp.where(own_group, state_in * jnp.exp(alast_x) + chunk_state, 0.0)

    lane = _iota((L, LANES), 1)
    group_of_lane = _iota((1, LANES), 1) // SSD_STATE
    heads = range(SSD_HEADS)
    hpg = SSD_HEADS // SSD_GROUPS
    cbs = [_dot(jnp.where(group_of_lane == g, cm, 0.0).astype(BF16), bm_t) for g in range(SSD_GROUPS)]
    segs = [jnp.where(causal, acum2[:, hh:hh + 1] - acum2_t[hh:hh + 1, :], NEG_BIG) for hh in heads]
    gs = [(cbs[hh // hpg] * jnp.exp2(segs[hh])).astype(BF16) for hh in heads]
    xps = [xdt[:, slot * LANES:(slot + 1) * LANES].astype(BF16) for slot in range(SSD_HEADS // 2)]
    res = [_dot(gs[hh], xps[hh // 2]) for hh in heads]
    parts = [jnp.where(lane < SSD_HEAD_DIM, res[2 * slot], res[2 * slot + 1]) for slot in range(SSD_HEADS // 2)]
    y = jnp.concatenate(parts, axis=1) + y_off + xs * dsk_ref[...]
    y = y * _silu(z)
    outs = []
    for g in range(SSD_GROUPS):
        yg = y[:, g * 256:(g + 1) * 256]
        outs.append(yg * lax.rsqrt(jnp.mean(yg * yg, axis=-1, keepdims=True) + NORM_EPS))
    y_ref[...] = (jnp.concatenate(outs, axis=1) * ng_ref[...]).astype(BF16)


def _gdn_init(xpad_ref, s_ref):
    xpad_ref[0:8, :] = jnp.zeros((8, GDN_CONV_DIM), F32)
    s_ref[...] = jnp.zeros_like(s_ref)


def _gdn_body(main_ref, small_ref, cw_ref, dtb_ref, alog_ref, ng_ref,
              y_ref, xpad_ref, s_ref, vnew_ref, sel_ref, *, T):
    qkv_raw = main_ref[:, 0:GDN_CONV_DIM]
    z = main_ref[:, GDN_CONV_DIM:GDN_MAIN]
    xa = _silu(_causal_conv(qkv_raw, xpad_ref, cw_ref, T))
    q = xa[:, 0:256]
    k = xa[:, 256:512]
    v = xa[:, 512:768]

    same_head = (_iota((256, 256), 0) // GDN_DK) == (_iota((256, 256), 1) // GDN_DK)
    head_ones = sel_ref["head_ones"][...]
    qn = q * lax.rsqrt(_dot_sel_right(q * q, head_ones) + NORM_EPS) * (GDN_DK ** -0.5)
    kn = k * lax.rsqrt(_dot_sel_right(k * k, head_ones) + NORM_EPS)

    sm = small_ref[...]
    beta = jax.nn.sigmoid(sm)
    gl = -jnp.exp(alog_ref[...]) * _softplus(sm + dtb_ref[...])
    ri = _iota((T, T), 0)
    ci = _iota((T, T), 1)
    same_chunk = (ri // CHUNK) == (ci // CHUNK)
    incl = jnp.logical_and(same_chunk, ri >= ci)
    strict = jnp.logical_and(same_chunk, ri > ci)
    gcum = _dot_sel_left(sel_ref["incl"][...], gl)
    beta_x = _dot_sel_right(beta, sel_ref["beta_expand"][...])
    gcum_x = _dot_sel_right(gcum, sel_ref["decay_expand"][...])
    glast_x = _dot_sel_left(sel_ref["chunk_last"][...], gcum_x)
    gcum2 = gcum * LOG2E
    gcum2_t = gcum2.T

    kb = kn * beta_x
    eg = jnp.exp(gcum_x)
    rhs = jnp.concatenate([v * beta_x, kb * eg], axis=1).astype(BF16)
    q_dec = (qn * eg).astype(BF16)
    k_dec_t = (kn * jnp.exp(glast_x - gcum_x)).T
    kn_t = kn.T.astype(BF16)
    head_of_lane = _iota((1, GDN_WIDTH), 1) // GDN_DK
    eye = (ri == ci).astype(F32)
    lower_left = []
    b = 1
    while b < CHUNK:
        in_block = (ri // (2 * b)) == (ci // (2 * b))
        lower_left.append(jnp.logical_and(in_block, (ri // b) % 2 > (ci // b) % 2))
        b *= 2

    heads = range(GDN_HEADS)
    hms = [head_of_lane == h for h in heads]
    gammas = [jnp.exp2(jnp.where(incl, gcum2[:, SMALL_A + h:SMALL_A + h + 1] - gcum2_t[SMALL_A + h:SMALL_A + h + 1, :],
                                 NEG_BIG)) for h in heads]
    kks = [_dot(jnp.where(hms[h], kb, 0.0).astype(BF16), kn_t) for h in heads]
    mlows = [jnp.where(strict, kks[h] * gammas[h], 0.0) for h in heads]
    ps = [eye - jnp.where(lower_left[0], mlows[h], 0.0) for h in heads]
    for ll in lower_left[1:]:
        pbs = [p.astype(BF16) for p in ps]
        ts = [_dot(pbs[h], jnp.where(ll, mlows[h], 0.0).astype(BF16)).astype(BF16) for h in heads]
        ps = [ps[h] - _dot(ts[h], pbs[h]) for h in heads]
    sols = [_dot(ps[h].astype(BF16), rhs) for h in heads]
    u = sols[0][:, 0:256]
    w = sols[0][:, 256:512]
    for h in range(1, GDN_HEADS):
        u = jnp.where(hms[h], sols[h][:, 0:256], u)
        w = jnp.where(hms[h], sols[h][:, 256:512], w)
    qks = [_dot(jnp.where(hms[h], qn, 0.0).astype(BF16), kn_t) for h in heads]
    qk_heads = [jnp.where(incl, qks[h] * gammas[h], 0.0).astype(BF16) for h in heads]

    vnew_ref[...] = jnp.zeros((T, GDN_WIDTH), F32)
    chunk_of_col = _iota((1, T), 1) // CHUNK
    o_inter = []
    for c in range(T // CHUNK):
        r0 = c * CHUNK
        s_in = s_ref[...]
        s_b = s_in.astype(BF16)
        vnew_ref[r0:r0 + CHUNK, :] = u[r0:r0 + CHUNK, :] - _dot(w[r0:r0 + CHUNK, :].astype(BF16), s_b)
        o_inter.append(_dot(q_dec[r0:r0 + CHUNK, :], s_b))
        kd = jnp.where(chunk_of_col == c, k_dec_t, 0.0).astype(BF16)
        upd = _dot(kd, vnew_ref[...].astype(BF16))
        dec = jnp.exp(glast_x[r0:r0 + 1, :])
        s_ref[...] = jnp.where(same_head, s_in * dec + upd, 0.0)

    vnew = vnew_ref[...].astype(BF16)
    o = jnp.concatenate(o_inter, axis=0)
    for h in range(GDN_HEADS):
        o = o + jnp.where(head_of_lane == h, _dot(qk_heads[h], vnew), 0.0)
    ms = _dot_sel_right(o * o, head_ones) * (1.0 / GDN_DV)
    y = o * lax.rsqrt(ms + NORM_EPS) * ng_ref[...] * _silu(z)
    y_ref[...] = y.astype(BF16)


_SELECTORS = ("causal", "incl", "chunk_last", "head_ones", "ssd_expand", "beta_expand", "decay_expand")


def _selector_shapes(T):
    return dict(causal=(T, T), incl=(T, T), chunk_last=(T, T), head_ones=(GDN_WIDTH, GDN_WIDTH),
                ssd_expand=(LANES, SSD_WIDTH), beta_expand=(LANES, GDN_WIDTH), decay_expand=(LANES, GDN_WIDTH))


def _selectors_init(sel_ref, T):
    ri = _iota((T, T), 0)
    ci = _iota((T, T), 1)
    same_chunk = (ri // CHUNK) == (ci // CHUNK)
    hr = _iota((GDN_WIDTH, GDN_WIDTH), 0) // GDN_DK
    hc = _iota((GDN_WIDTH, GDN_WIDTH), 1) // GDN_DK
    er = _iota((LANES, GDN_WIDTH), 0)
    ec = _iota((LANES, GDN_WIDTH), 1) // GDN_DK
    masks = dict(
        causal=ri >= ci,
        incl=jnp.logical_and(same_chunk, ri >= ci),
        chunk_last=ci == (ri // CHUNK) * CHUNK + (CHUNK - 1),
        head_ones=hr == hc,
        ssd_expand=_iota((LANES, SSD_WIDTH), 0) == _iota((LANES, SSD_WIDTH), 1) // SSD_HEAD_DIM,
        beta_expand=er == ec + SMALL_B,
        decay_expand=er == ec + SMALL_A,
    )
    for name in _SELECTORS:
        sel_ref[name][...] = masks[name].astype(F32).astype(BF16)


def _ssd_gdn_body(ssd_main, small, ssd_cw, ssd_cb, ssd_dtb, ssd_alog, ssd_dsk, ssd_ng,
                  gdn_main, gdn_cw, gdn_dtb, gdn_alog, gdn_ng,
                  y_ssd, y_gdn, ssd_xpad, ssd_state, gdn_xpad, gdn_state, gdn_vnew, *sel_refs, T):
    sel_ref = dict(zip(_SELECTORS, sel_refs))

    @pl.when(pl.program_id(1) == 0)
    def _():
        _ssd_init(ssd_xpad, ssd_state)
        _gdn_init(gdn_xpad, gdn_state)
        _selectors_init(sel_ref, T)

    _ssd_body(ssd_main, small, ssd_cw, ssd_cb, ssd_dtb, ssd_alog, ssd_dsk, ssd_ng, y_ssd, ssd_xpad, ssd_state,
              sel_ref, L=T)
    _gdn_body(gdn_main, small, gdn_cw, gdn_dtb, gdn_alog, gdn_ng, y_gdn, gdn_xpad, gdn_state, gdn_vnew,
              sel_ref, T=T)


def _ssd_gdn(ssd_main, small, ssd_cw, ssd_cb, ssd_dtb, ssd_alog, ssd_dsk, ssd_ng,
             gdn_main, gdn_cw, gdn_dtb, gdn_alog, gdn_ng, bsz, seq, T):
    T = min(T, seq)
    nt = seq // T
    row = lambda b, t: (b * nt + t, 0)
    fixed = lambda b, t: (0, 0)
    return pl.pallas_call(
        functools.partial(_ssd_gdn_body, T=T),
        grid=(bsz, nt),
        in_specs=[
            pl.BlockSpec((T, SSD_MAIN), row),
            pl.BlockSpec((T, LANES), row),
            pl.BlockSpec((CONV_K, SSD_CONV_DIM), fixed),
            pl.BlockSpec((1, SSD_CONV_DIM), fixed),
            pl.BlockSpec((1, LANES), fixed),
            pl.BlockSpec((1, LANES), fixed),
            pl.BlockSpec((1, SSD_WIDTH), fixed),
            pl.BlockSpec((1, SSD_WIDTH), fixed),
            pl.BlockSpec((T, GDN_MAIN), row),
            pl.BlockSpec((CONV_K, GDN_CONV_DIM), fixed),
            pl.BlockSpec((1, LANES), fixed),
            pl.BlockSpec((1, LANES), fixed),
            pl.BlockSpec((1, GDN_WIDTH), fixed),
        ],
        out_specs=[pl.BlockSpec((T, SSD_WIDTH), row), pl.BlockSpec((T, GDN_WIDTH), row)],
        out_shape=[jax.ShapeDtypeStruct((bsz * seq, SSD_WIDTH), BF16),
                   jax.ShapeDtypeStruct((bsz * seq, GDN_WIDTH), BF16)],
        scratch_shapes=[
            pltpu.VMEM((8 + T, SSD_CONV_DIM), F32),
            pltpu.VMEM((LANES, SSD_WIDTH), F32),
            pltpu.VMEM((8 + T, GDN_CONV_DIM), F32),
            pltpu.VMEM((GDN_WIDTH, GDN_WIDTH), F32),
            pltpu.VMEM((T, GDN_WIDTH), F32),
        ] + [pltpu.VMEM(_selector_shapes(T)[name], BF16) for name in _SELECTORS],
        compiler_params=_params(("parallel", "arbitrary")),
        name="ssd_gdn",
    )(ssd_main, small, ssd_cw, ssd_cb, ssd_dtb, ssd_alog, ssd_dsk, ssd_ng, gdn_main, gdn_cw, gdn_dtb, gdn_alog, gdn_ng)


def _mla_attn_body(qt_ref, k_ref, vt_ref, o_ref, *, tq, tk):
    i = pl.program_id(1)
    sub = tq // tk
    qts = [jnp.concatenate([qt_ref[0, d, h * HEAD_PAD:(h + 1) * HEAD_PAD, :] for d in range(qt_ref.shape[1])], axis=1)
           for h in range(MLA_HEADS)]

    def step(j, carry, diag_sub):
        kblk = k_ref[0, pl.ds(pl.multiple_of(j * tk, tk), tk), :]
        if diag_sub is not None:
            visible = ((_iota((tk, tq), 0) + diag_sub * tk) // CHUNK) <= (_iota((tk, tq), 1) // CHUNK)
        out = []
        for h0 in range(0, MLA_HEADS, MLA_INTERLEAVE):
            heads = range(h0, h0 + MLA_INTERLEAVE)
            ss = {h: _dot(kblk[:, h * HEAD_PAD:(h + 1) * HEAD_PAD], qts[h]) for h in heads}
            if diag_sub is not None:
                ss = {h: jnp.where(visible, ss[h], NEG_BIG) for h in heads}
            m_new = {h: jnp.maximum(carry[h][0], jnp.max(ss[h], axis=0, keepdims=True)) for h in heads}
            ps = {h: jnp.exp2(ss[h] - m_new[h]).astype(BF16) for h in heads}
            pv = {h: _dot(vt_ref[0, j, h * V_ROWS:(h + 1) * V_ROWS, :], ps[h]) for h in heads}
            out.extend((m_new[h], jnp.exp2(carry[h][0] - m_new[h]) * carry[h][1] + pv[h]) for h in heads)
        return tuple(out)

    init = tuple((jnp.full((1, tq), NEG_BIG, F32), jnp.zeros((V_ROWS, tq), F32)) for _ in range(MLA_HEADS))
    carry = lax.fori_loop(0, i * sub, functools.partial(step, diag_sub=None), init)
    for d in range(sub):
        carry = step(i * sub + d, carry, d)
    outs = []
    for h in range(MLA_HEADS):
        acc = carry[h][1]
        normed = acc / acc[MLA_V:MLA_V + 1, :]
        outs.append(jnp.concatenate([normed, jnp.zeros((HEAD_PAD - V_ROWS, tq), F32)], axis=0).T)
    o_ref[...] = jnp.concatenate(outs, axis=1).astype(BF16)


def _mla_attn(qt, k, vt, bsz, seq, tq, tk):
    tq = min(tq, seq)
    nq = seq // tq
    width = MLA_HEADS * HEAD_PAD
    return pl.pallas_call(
        functools.partial(_mla_attn_body, tq=tq, tk=tk),
        grid=(bsz, nq),
        in_specs=[
            pl.BlockSpec((1, tq // qt.shape[3], width, qt.shape[3]), lambda b, i: (b, i, 0, 0)),
            pl.BlockSpec((1, seq, width), lambda b, i: (b, 0, 0)),
            pl.BlockSpec((1, seq // tk, MLA_HEADS * V_ROWS, tk), lambda b, i: (b, 0, 0, 0)),
        ],
        out_specs=pl.BlockSpec((tq, width), lambda b, i: (b * nq + i, 0)),
        out_shape=jax.ShapeDtypeStruct((bsz * seq, width), BF16),
        compiler_params=_params(("parallel", "arbitrary")),
        name="mla_attn",
    )(qt, k.reshape(bsz, seq, width), vt)


def _memkv_body(mem_ref, g_ref, w_ref, kt_ref, v_ref):
    mn = _rms(mem_ref[...], g_ref[...]).astype(BF16)
    kv = _dot(mn, w_ref[...])
    kt_ref[0] = kv[:, 0:D_MODEL].T.astype(BF16)
    v_ref[0] = kv[:, D_MODEL:2 * D_MODEL].astype(BF16)


def _memkv(mem2d, g, w, bsz):
    return pl.pallas_call(
        _memkv_body,
        grid=(bsz,),
        in_specs=[
            pl.BlockSpec((N_MEM, D_MODEL), lambda b: (b, 0)),
            pl.BlockSpec((1, D_MODEL), lambda b: (0, 0)),
            pl.BlockSpec((D_MODEL, 2 * D_MODEL), lambda b: (0, 0)),
        ],
        out_specs=[
            pl.BlockSpec((1, D_MODEL, N_MEM), lambda b: (b, 0, 0)),
            pl.BlockSpec((1, N_MEM, D_MODEL), lambda b: (b, 0, 0)),
        ],
        out_shape=[
            jax.ShapeDtypeStruct((bsz, D_MODEL, N_MEM), BF16),
            jax.ShapeDtypeStruct((bsz, N_MEM, D_MODEL), BF16),
        ],
        compiler_params=_params(("parallel",)),
        name="memkv",
    )(mem2d, g, w)


def _post_body(h_ref, ys_ref, ym_ref, yg_ref, wos_ref, wom_ref, wog_ref, gmix_ref, gxa_ref,
               wq_ref, kt_ref, v_ref, wo_ref, gpost_ref, o_ref):
    mix = _dot(ys_ref[...], wos_ref[...]) + _dot(ym_ref[...], wom_ref[...]) + _dot(yg_ref[...], wog_ref[...])
    h1 = h_ref[...] + _rms(mix, gmix_ref[...])
    u = _rms(h1, gxa_ref[...]).astype(BF16)
    q = (_dot(u, wq_ref[...]) * (XA_HEAD_DIM ** -0.5)).astype(BF16)
    heads = range(XA_HEADS)
    w = XA_HEAD_DIM
    ss = [_dot(q[:, hd * w:(hd + 1) * w], kt_ref[0, hd * w:(hd + 1) * w, :]) for hd in heads]
    ps = [jnp.exp(s - jnp.max(s, axis=-1, keepdims=True)) for s in ss]
    os_ = [_dot(ps[hd].astype(BF16), v_ref[0, :, hd * w:(hd + 1) * w]) for hd in heads]
    outs = [(os_[hd] / jnp.sum(ps[hd], axis=-1, keepdims=True)).astype(BF16) for hd in heads]
    xa = _dot(jnp.concatenate(outs, axis=1), wo_ref[...])
    o_ref[...] = h1 + _rms(xa, gpost_ref[...])


def _post(h, ys, ym, yg, wos, wom, wog, gmix, gxa, wq, kt, v, wo, gpost, bsz, seq, tm):
    tm = min(tm, seq)
    nt = seq // tm
    row = lambda b, t: (b * nt + t, 0)
    fixed = lambda b, t: (0, 0)
    full = lambda a: pl.BlockSpec(a.shape, fixed)
    return pl.pallas_call(
        _post_body,
        grid=(bsz, nt),
        in_specs=[
            pl.BlockSpec((tm, D_MODEL), row),
            pl.BlockSpec((tm, ys.shape[1]), row),
            pl.BlockSpec((tm, ym.shape[1]), row),
            pl.BlockSpec((tm, yg.shape[1]), row),
            full(wos), full(wom), full(wog), full(gmix), full(gxa), full(wq),
            pl.BlockSpec((1, D_MODEL, N_MEM), lambda b, t: (b, 0, 0)),
            pl.BlockSpec((1, N_MEM, D_MODEL), lambda b, t: (b, 0, 0)),
            full(wo), full(gpost),
        ],
        out_specs=pl.BlockSpec((tm, D_MODEL), row),
        out_shape=jax.ShapeDtypeStruct((bsz * seq, D_MODEL), F32),
        compiler_params=_params(("parallel", "arbitrary")),
        name="mix_out_xattn",
    )(h, ys, ym, yg, wos, wom, wog, gmix, gxa, wq, kt, v, wo, gpost)


def _pad_lanes(vals, offset, width=LANES):
    n = vals.shape[-1]
    pad = [(0, 0)] * (vals.ndim - 1) + [(offset, width - offset - n)]
    return jnp.pad(vals, pad)[..., None, :]


def _prep_params(norm_g, ffn_w_up, ffn_w_down, w_in, ssd_conv_w, ssd_conv_b, ssd_dt_bias, ssd_a_log,
                 ssd_d, ssd_norm_g, mla_q_norm_g, mla_w_uq, mla_kv_norm_g, mla_w_ukv, gdn_conv_w,
                 gdn_dt_bias, gdn_a_log, gdn_norm_g, w_out, xa_w_q, xa_w_kv, xa_w_o):
    depth = norm_g.shape[0]
    vone = np.zeros((1, MLA_HEADS * HEAD_PAD), np.float32)
    vone[0, np.arange(MLA_HEADS) * HEAD_PAD + MLA_V] = 1.0
    return dict(
        norm_g=norm_g[:, :, None, :],
        w_up=ffn_w_up.astype(BF16),
        wd=ffn_w_down.astype(BF16),
        w_in=_gather_cols(w_in, _IN_PERM).astype(BF16),
        ssd_cw=ssd_conv_w,
        ssd_cb=ssd_conv_b[:, None, :],
        ssd_dtb=_pad_lanes(ssd_dt_bias, SMALL_DT),
        ssd_alog=_pad_lanes(ssd_a_log, SMALL_DT),
        ssd_dsk=jnp.repeat(ssd_d, SSD_HEAD_DIM, axis=-1)[:, None, :],
        ssd_ng=ssd_norm_g[:, None, :],
        mla_gq=mla_q_norm_g[:, None, :],
        mla_gkv=mla_kv_norm_g[:, None, :],
        mla_wq=_gather_cols(mla_w_uq, _UQ_PERM).astype(BF16),
        mla_wkv=_gather_cols(mla_w_ukv, _UKV_PERM).astype(BF16),
        mla_vone=jnp.broadcast_to(jnp.asarray(vone), (depth,) + vone.shape),
        gdn_cw=gdn_conv_w,
        gdn_dtb=_pad_lanes(gdn_dt_bias, SMALL_A),
        gdn_alog=_pad_lanes(gdn_a_log, SMALL_A),
        gdn_ng=jnp.tile(gdn_norm_g, (1, GDN_HEADS))[:, None, :],
        wo_ssd=w_out[:, 0:SSD_WIDTH, :].astype(BF16),
        wo_mla=jnp.swapaxes(_gather_cols(jnp.swapaxes(w_out, 1, 2), _WOUT_MLA_ROWS), 1, 2).astype(BF16),
        wo_gdn=w_out[:, SSD_WIDTH + MLA_HEADS * MLA_V:, :].astype(BF16),
        xa_wq=xa_w_q.astype(BF16),
        xa_wkv=xa_w_kv.astype(BF16),
        xa_wo=xa_w_o.astype(BF16),
    )


def _rope_tables(positions):
    inv = 1.0 / (ROPE_THETA ** (jnp.arange(0, MLA_ROPE, 2, dtype=F32) / MLA_ROPE))
    ang = positions.astype(F32).reshape(-1, 1) * inv
    cos, sin = jnp.cos(ang), jnp.sin(ang)
    n = ang.shape[0]
    ones = jnp.ones((n, MLA_NOPE), F32)
    zeros_nope = jnp.zeros((n, MLA_NOPE), F32)
    zeros_pad = jnp.zeros((n, HEAD_PAD - MLA_NOPE - MLA_ROPE), F32)
    cos_t = jnp.concatenate([ones, cos, cos, zeros_pad], axis=1)
    sin_t = jnp.concatenate([zeros_nope, -sin, sin, zeros_pad], axis=1)
    return cos_t, sin_t


FFN_TM = 512
TOKEN_TM = 512
SEQ_TILE = 256
MLA_TK = 512
MLA_TQ = 512
MLA_INTERLEAVE = 4


def kernel(x, mem, positions, norm_g, ffn_w_up, ffn_w_down, w_in, ssd_conv_w, ssd_conv_b, ssd_dt_bias,
           ssd_a_log, ssd_d, ssd_norm_g, mla_q_norm_g, mla_w_uq, mla_kv_norm_g, mla_w_ukv, gdn_conv_w,
           gdn_dt_bias, gdn_a_log, gdn_norm_g, w_out, xa_w_q, xa_w_kv, xa_w_o):
    bsz, seq, _ = x.shape
    n = bsz * seq
    cos_t, sin_t = _rope_tables(positions)
    weights = (norm_g, ffn_w_up, ffn_w_down, w_in, ssd_conv_w, ssd_conv_b, ssd_dt_bias,
               ssd_a_log, ssd_d, ssd_norm_g, mla_q_norm_g, mla_w_uq, mla_kv_norm_g, mla_w_ukv,
               gdn_conv_w, gdn_dt_bias, gdn_a_log, gdn_norm_g, w_out, xa_w_q, xa_w_kv, xa_w_o)
    mem2d = mem.reshape(bsz * N_MEM, D_MODEL)

    def layer(h, p):
        g = p["norm_g"]
        h = _ffn(h, g[FFN1_PRE], p["w_up"][0], p["wd"][0], g[FFN1_POST], FFN_TM)
        c_ssd, c_gdn, c_small, qt, kk, vt = _inproj(h, g[MIX_PRE], p["w_in"], cos_t, sin_t, p["mla_gq"], p["mla_gkv"],
                                                    p["mla_wq"], p["mla_wkv"], p["mla_vone"], bsz, seq, TOKEN_TM)
        y_ssd, y_gdn = _ssd_gdn(c_ssd, c_small, p["ssd_cw"], p["ssd_cb"], p["ssd_dtb"], p["ssd_alog"], p["ssd_dsk"],
                                p["ssd_ng"], c_gdn, p["gdn_cw"], p["gdn_dtb"], p["gdn_alog"], p["gdn_ng"],
                                bsz, seq, SEQ_TILE)
        y_mla = _mla_attn(qt, kk, vt, bsz, seq, MLA_TQ, MLA_TK)
        mkt, mv = _memkv(mem2d, g[MEM_NORM], p["xa_wkv"], bsz)
        h = _post(h, y_ssd, y_mla, y_gdn, p["wo_ssd"], p["wo_mla"], p["wo_gdn"], g[MIX_POST], g[XA_PRE],
                  p["xa_wq"], mkt, mv, p["xa_wo"], g[XA_POST], bsz, seq, TOKEN_TM)
        h = _ffn(h, g[FFN2_PRE], p["w_up"][1], p["wd"][1], g[FFN2_POST], FFN_TM)
        return h

    h = x.reshape(n, D_MODEL)
    for i in range(norm_g.shape[0]):
        params = _prep_params(*(w[i:i + 1] for w in weights))
        h = layer(h, jax.tree.map(lambda a: a[0], params))
    return h.reshape(bsz, seq, D_MODEL)
```
